```python
import jax
import jax.numpy as jnp
from jax import lax
import numpy as np

D_MODEL = 2048
BATCH = 2
SEQ = 8192
DEPTH = 1
DEC_BATCH = 16
DEC_SEQ = 2048
PAST_LEN = 128

HEAD_DIM = 128
DN_HEADS = 8
DN_WIDTH = DN_HEADS * HEAD_DIM
DN_CONV = 5
DN_CHUNK = 64
ATTN_GROUPS = ((128, 1), (512, 4), (2048, 16))
N_GROUPS = 3
ATTN_HEADS_PER_GROUP = 4
ATTN_HEADS = N_GROUPS * ATTN_HEADS_PER_GROUP
ATTN_WIDTH = ATTN_HEADS * HEAD_DIM
ATTN_OUT_WIDTH = ATTN_HEADS_PER_GROUP * HEAD_DIM
ROPE_THETA = 10000.0
D_FF = 4 * D_MODEL
N_BRANCH = 2
N_MOD = 6
NORM_EPS = 1e-6
MASK_VALUE = -1e30
IN_SPLIT_SIZES = (3 * DN_WIDTH, DN_WIDTH, 2 * DN_HEADS, 2 * DN_HEADS, 3 * ATTN_WIDTH, N_BRANCH * D_MODEL)
IN_COLS = 3 * DN_WIDTH + DN_WIDTH + 2 * DN_HEADS + 2 * DN_HEADS + 3 * ATTN_WIDTH + N_BRANCH * D_MODEL

kernel_name = 'hybrid_deltanet_dilated_attn_encoder'

F32 = jnp.float32


def rmsnorm(x, w):
    xf = x.astype(F32)
    y = xf * lax.rsqrt(jnp.mean(xf * xf, axis=-1, keepdims=True) + NORM_EPS) * w.astype(F32)
    return y.astype(x.dtype)


def l2norm(t):
    return t * lax.rsqrt(jnp.sum(t * t, axis=-1, keepdims=True) + NORM_EPS)


def rotary(x, pos):
    half = HEAD_DIM // 2
    inv_freq = ROPE_THETA ** (-jnp.arange(half, dtype=F32) / half)
    ang = pos.astype(F32)[:, None] * inv_freq[None, :]
    cos = jnp.cos(ang)[None, :, None, :]
    sin = jnp.sin(ang)[None, :, None, :]
    xf = x.astype(F32)
    x1, x2 = xf[..., :half], xf[..., half:]
    return jnp.concatenate([x1 * cos - x2 * sin, x1 * sin + x2 * cos], axis=-1)


def centred_depthwise_conv(x, w):
    C = x.shape[-1]
    pad = DN_CONV // 2
    return lax.conv_general_dilated(
        x, w.astype(x.dtype)[:, None, :], window_strides=(1,), padding=((pad, pad),),
        dimension_numbers=('NWC', 'WIO', 'NWC'), feature_group_count=C)


def gated_delta_chunked(q, k, v, g, beta):
    B, H, L, Dk = k.shape
    Dv = v.shape[-1]
    C = DN_CHUNK
    N = L // C
    q = q.reshape(B, H, N, C, Dk)
    k = k.reshape(B, H, N, C, Dk)
    v = v.reshape(B, H, N, C, Dv)
    g = g.reshape(B, H, N, C)
    beta = beta.reshape(B, H, N, C)
    G = jnp.cumsum(g, axis=-1)
    tri = jnp.tril(jnp.ones((C, C), dtype=bool))
    strict = jnp.tril(jnp.ones((C, C), dtype=bool), -1)
    diff = G[..., :, None] - G[..., None, :]
    decay = jnp.where(tri, jnp.exp(jnp.where(tri, diff, 0.0)), 0.0)
    kb = k * beta[..., None]
    lmat = jnp.where(strict, jnp.einsum('bhnid,bhnjd->bhnij', kb, k) * decay, 0.0)
    eye = jnp.eye(C, dtype=F32)
    tmat = lax.linalg.triangular_solve(lmat, jnp.broadcast_to(eye, lmat.shape),
                                       left_side=True, lower=True, unit_diagonal=True)
    u = jnp.einsum('bhnij,bhnjv->bhniv', tmat, v * beta[..., None])
    w = jnp.einsum('bhnij,bhnjk->bhnik', tmat, kb * jnp.exp(G)[..., None])
    qk = jnp.where(tri, jnp.einsum('bhnid,bhnjd->bhnij', q, k) * decay, 0.0)
    qg = q * jnp.exp(G)[..., None]
    kdec = k * jnp.exp(G[..., -1:] - G)[..., None]
    glast = jnp.exp(G[..., -1])

    def step(S, xs):
        u_c, w_c, qk_c, qg_c, kd_c, gl_c = xs
        v_new = u_c - jnp.einsum('bhck,bhkv->bhcv', w_c, S)
        o = jnp.einsum('bhck,bhkv->bhcv', qg_c, S) + jnp.einsum('bhij,bhjv->bhiv', qk_c, v_new)
        S = S * gl_c[..., None, None] + jnp.einsum('bhck,bhcv->bhkv', kd_c, v_new)
        return S, o

    xs = tuple(jnp.moveaxis(t, 2, 0) for t in (u, w, qk, qg, kdec, glast))
    S0 = jnp.zeros((B, H, Dk, Dv), F32)
    _, o = lax.scan(step, S0, xs)
    return jnp.moveaxis(o, 0, 2).reshape(B, H, L, Dv)


def deltanet_branch(qkv, z, a, b, conv_w, A_log, dt_bias, norm_w):
    B, L, _ = qkv.shape
    qkv = jax.nn.silu(centred_depthwise_conv(qkv, conv_w)).astype(F32)
    q, k, v = [t.reshape(B, L, DN_HEADS, HEAD_DIM).transpose(0, 2, 1, 3) for t in jnp.split(qkv, 3, axis=-1)]
    q = l2norm(q) * (HEAD_DIM ** -0.5)
    k = l2norm(k)
    log_decay = -jnp.exp(A_log.astype(F32)) * jax.nn.softplus(a.astype(F32) + dt_bias.astype(F32))
    beta = jax.nn.sigmoid(b.astype(F32))
    log_decay = log_decay.transpose(2, 0, 3, 1)
    beta = beta.transpose(2, 0, 3, 1)
    o_fwd = gated_delta_chunked(q, k, v, log_decay[0], beta[0])
    flip = lambda t: jnp.flip(t, axis=2)
    o_bwd = flip(gated_delta_chunked(flip(q), flip(k), flip(v), flip(log_decay[1]), flip(beta[1])))
    o = (o_fwd + o_bwd).transpose(0, 2, 1, 3)
    o = rmsnorm(o, norm_w) * jax.nn.silu(z.astype(F32).reshape(B, L, DN_HEADS, HEAD_DIM))
    return o.reshape(B, L, DN_WIDTH).astype(z.dtype)


def dilated_window_attention(q, k, v, window, dilation):
    B, L, H, Dh = q.shape
    n_side = window // (2 * dilation)
    blk = n_side
    M = L // dilation
    nb = -(-M // blk)
    Mp = nb * blk

    def to_residues(t):
        t = t.reshape(B, M, dilation, H, Dh).transpose(0, 2, 3, 1, 4)
        return jnp.pad(t, ((0, 0), (0, 0), (0, 0), (0, Mp - M), (0, 0)))

    def windows(t):
        tp = jnp.pad(t, ((0, 0), (0, 0), (0, 0), (blk, blk), (0, 0))).reshape(B, dilation, H, nb + 2, blk, Dh)
        return jnp.concatenate([tp[:, :, :, :-2], tp[:, :, :, 1:-1], tp[:, :, :, 2:]], axis=4)

    qb = to_residues(q).reshape(B, dilation, H, nb, blk, Dh)
    kw = windows(to_residues(k))
    vw = windows(to_residues(v))
    s = jnp.einsum('brhnqd,brhnkd->brhnqk', qb, kw) * (Dh ** -0.5)
    qpos = jnp.arange(nb)[:, None] * blk + jnp.arange(blk)[None, :]
    kpos = jnp.arange(nb)[:, None] * blk - blk + jnp.arange(3 * blk)[None, :]
    valid = ((jnp.abs(qpos[:, :, None] - kpos[:, None, :]) <= n_side)
             & (kpos[:, None, :] >= 0) & (kpos[:, None, :] < M))
    s = jnp.where(valid, s, MASK_VALUE)
    m = jnp.max(s, axis=-1, keepdims=True)
    p = jnp.exp(s - m)
    den = jnp.sum(p, axis=-1, keepdims=True)
    o = jnp.einsum('brhnqk,brhnkd->brhnqd', p, vw) / den
    lse = (m + jnp.log(den))[..., 0]
    o = o.reshape(B, dilation, H, Mp, Dh)[:, :, :, :M].transpose(0, 3, 1, 2, 4).reshape(B, L, H, Dh)
    lse = lse.reshape(B, dilation, H, Mp)[:, :, :, :M].transpose(0, 3, 1, 2).reshape(B, L, H)
    return o, lse


def attention_branch(qkv):
    B, L, _ = qkv.shape
    pos = jnp.arange(L)
    q, k, v = [t.reshape(B, L, ATTN_HEADS, HEAD_DIM) for t in jnp.split(qkv, 3, axis=-1)]
    q, k, v = rotary(q, pos), rotary(k, pos), v.astype(F32)
    outs, lses = [], []
    for gi, (window, dilation) in enumerate(ATTN_GROUPS):
        hs = slice(gi * ATTN_HEADS_PER_GROUP, (gi + 1) * ATTN_HEADS_PER_GROUP)
        o, lse = dilated_window_attention(q[:, :, hs], k[:, :, hs], v[:, :, hs], window, dilation)
        outs.append(o)
        lses.append(lse)
    wts = jax.nn.softmax(jnp.stack(lses), axis=0)
    o = jnp.sum(wts[..., None] * jnp.stack(outs), axis=0)
    return o.reshape(B, L, ATTN_OUT_WIDTH).astype(qkv.dtype)


def encoder_layer(x, c, w_ada, b_ada, norm_pre_mix, norm_post_mix, norm_pre_ffn, norm_post_ffn,
                  w_in, conv_w, A_log, dt_bias, dn_norm_w, w_dn_out, w_at_out, w_out, w_ff1, w_ff2):
    B, L, _ = x.shape
    mod = jax.nn.silu(c) @ w_ada + b_ada
    shift1, scale1, gate1, shift2, scale2, gate2 = jnp.split(mod[:, None, :], N_MOD, axis=-1)
    h = rmsnorm(x, norm_pre_mix) * (1.0 + scale1) + shift1
    proj = h @ w_in
    split_at = np.cumsum(IN_SPLIT_SIZES)[:-1].tolist()
    dn_qkv, dn_z, dn_a, dn_b, at_qkv, merge = jnp.split(proj, split_at, axis=-1)
    y_dn = deltanet_branch(dn_qkv, dn_z, dn_a.reshape(B, L, 2, DN_HEADS), dn_b.reshape(B, L, 2, DN_HEADS),
                           conv_w, A_log, dt_bias, dn_norm_w) @ w_dn_out
    y_at = attention_branch(at_qkv) @ w_at_out
    g_dn, g_at = jnp.split(jax.nn.sigmoid(merge), N_BRANCH, axis=-1)
    mixed = (g_dn * y_dn + g_at * y_at) @ w_out
    x = x + gate1 * rmsnorm(mixed, norm_post_mix)
    h2 = rmsnorm(x, norm_pre_ffn) * (1.0 + scale2) + shift2
    f = jnp.square(jax.nn.relu(h2 @ w_ff1)) @ w_ff2
    return x + gate2 * rmsnorm(f, norm_post_ffn)


def trunk(x, c, w_ada, b_ada, norm_pre_mix, norm_post_mix, norm_pre_ffn, norm_post_ffn,
          w_in, conv_w, A_log, dt_bias, dn_norm_w, w_dn_out, w_at_out, w_out, w_ff1, w_ff2):
    for l in range(DEPTH):
        x = encoder_layer(x, c, w_ada[l], b_ada[l], norm_pre_mix[l], norm_post_mix[l], norm_pre_ffn[l],
                          norm_post_ffn[l], w_in[l], conv_w[l], A_log[l], dt_bias[l], dn_norm_w[l],
                          w_dn_out[l], w_at_out[l], w_out[l], w_ff1[l], w_ff2[l])
    return x


def setup_inputs(seed: int = 0) -> dict:
    key = jax.random.key(seed)
    ks = jax.random.split(key, 24)

    def dense(k, shape, fan_in, scale=1.0):
        return jax.random.normal(k, shape, F32) * (scale * fan_in ** -0.5)

    def gain(k, shape):
        return 1.0 + 0.02 * jax.random.normal(k, shape, F32)

    dt = jnp.exp(jax.random.uniform(ks[12], (DEPTH, 2, DN_HEADS), F32, np.log(1e-3), np.log(1e-1)))
    return {
        'x_prompt': jax.random.normal(ks[0], (BATCH, SEQ, D_MODEL), F32),
        'x_sample': jax.random.normal(ks[1], (DEC_BATCH, DEC_SEQ, D_MODEL), F32),
        'c_prompt': jax.random.normal(ks[2], (BATCH, D_MODEL), F32),
        'c_sample': jax.random.normal(ks[3], (DEC_BATCH, D_MODEL), F32),
        'w_ada': dense(ks[4], (DEPTH, D_MODEL, N_MOD * D_MODEL), D_MODEL, 0.5),
        'b_ada': 0.01 * jax.random.normal(ks[5], (DEPTH, N_MOD * D_MODEL), F32),
        'norm_pre_mix': gain(ks[6], (DEPTH, D_MODEL)),
        'norm_post_mix': gain(ks[7], (DEPTH, D_MODEL)),
        'norm_pre_ffn': gain(ks[8], (DEPTH, D_MODEL)),
        'norm_post_ffn': gain(ks[9], (DEPTH, D_MODEL)),
        'w_in': dense(ks[10], (DEPTH, D_MODEL, IN_COLS), D_MODEL),
        'conv_w': dense(ks[11], (DEPTH, DN_CONV, 3 * DN_WIDTH), DN_CONV),
        'A_log': jnp.log(jax.random.uniform(ks[13], (DEPTH, 2, DN_HEADS), F32, 1.0, 16.0)),
        'dt_bias': dt + jnp.log(-jnp.expm1(-dt)),
        'dn_norm_w': gain(ks[14], (DEPTH, HEAD_DIM)),
        'w_dn_out': dense(ks[15], (DEPTH, DN_WIDTH, D_MODEL), DN_WIDTH),
        'w_at_out': dense(ks[16], (DEPTH, ATTN_OUT_WIDTH, D_MODEL), ATTN_OUT_WIDTH),
        'w_out': dense(ks[17], (DEPTH, D_MODEL, D_MODEL), D_MODEL),
        'w_ff1': dense(ks[18], (DEPTH, D_MODEL, D_FF), D_MODEL),
        'w_ff2': dense(ks[19], (DEPTH, D_FF, D_MODEL), D_FF),
    }


def reference(x_prompt, x_sample, c_prompt, c_sample, w_ada, b_ada, norm_pre_mix, norm_post_mix,
              norm_pre_ffn, norm_post_ffn, w_in, conv_w, A_log, dt_bias, dn_norm_w, w_dn_out,
              w_at_out, w_out, w_ff1, w_ff2):
    y_prompt = trunk(x_prompt, c_prompt, w_ada, b_ada, norm_pre_mix, norm_post_mix, norm_pre_ffn,
                     norm_post_ffn, w_in, conv_w, A_log, dt_bias, dn_norm_w, w_dn_out, w_at_out,
                     w_out, w_ff1, w_ff2)
    y_sample = trunk(x_sample, c_sample, w_ada, b_ada, norm_pre_mix, norm_post_mix, norm_pre_ffn,
                     norm_post_ffn, w_in, conv_w, A_log, dt_bias, dn_norm_w, w_dn_out, w_at_out,
                     w_out, w_ff1, w_ff2)
    return (y_prompt, y_sample)
```

```python
import functools

import jax
import jax.numpy as jnp
from jax import lax
from jax.experimental import pallas as pl
from jax.experimental.pallas import tpu as pltpu

F32 = jnp.float32
BF16 = jnp.bfloat16

D_MODEL = 2048
HEAD_DIM = 128
DN_HEADS = 8
DN_WIDTH = DN_HEADS * HEAD_DIM
DN_CONV = 5
DN_CHUNK = 64
ATTN_GROUPS = ((128, 1), (512, 4), (2048, 16))
N_GROUPS = 3
HPG = 4
ATTN_HEADS = N_GROUPS * HPG
ATTN_WIDTH = ATTN_HEADS * HEAD_DIM
ATTN_OUT = HPG * HEAD_DIM
ROPE_THETA = 10000.0
D_FF = 4 * D_MODEL
N_MOD = 6
NORM_EPS = 1e-6
MASK_VALUE = -1e30

DN_HEAD_COLS = 3 * HEAD_DIM
COL_DN = 0
COL_Z = COL_DN + 3 * DN_WIDTH
COL_MERGE = COL_Z + DN_WIDTH
COL_AT = COL_MERGE + 2 * D_MODEL
AT_GROUP_COLS = 3 * ATTN_OUT
W_MAIN = COL_AT + 3 * ATTN_WIDTH
AB_COLS = 128

W_PROJ = COL_AT + AT_GROUP_COLS

PROJ_TN = 512
AT_TILE0 = COL_AT // PROJ_TN
MAIN_TILES = W_PROJ // PROJ_TN
HALO = 64

VMEM_LIMIT = 56 * 1024 * 1024


def _cparams(sem):
    return pltpu.CompilerParams(dimension_semantics=sem, vmem_limit_bytes=VMEM_LIMIT)


def _dot(a, b):
    return jnp.dot(a, b, preferred_element_type=F32)


def _dot_nt(a, b):
    return lax.dot_general(a, b, (((1,), (1,)), ((), ())), preferred_element_type=F32)


def _dot_tn(a, b):
    return lax.dot_general(a, b, (((0,), (0,)), ((), ())), preferred_element_type=F32)


def _silu(x):
    return x * jax.nn.sigmoid(x)


def _mod_kernel(c_ref, w_ref, b_ref, o_ref):
    s = _silu(c_ref[...]).astype(BF16)
    o_ref[...] = _dot(s, w_ref[...].astype(BF16)) + b_ref[...]


def _ada_mod(c_all, w_ada, b_ada):
    nb, n = c_all.shape[0], w_ada.shape[1]
    tn = 512
    return pl.pallas_call(
        _mod_kernel,
        grid=(n // tn,),
        in_specs=[pl.BlockSpec((nb, D_MODEL), lambda j: (0, 0)),
                  pl.BlockSpec((D_MODEL, tn), lambda j: (0, j)),
                  pl.BlockSpec((1, tn), lambda j: (0, j))],
        out_specs=pl.BlockSpec((nb, tn), lambda j: (0, j)),
        out_shape=jax.ShapeDtypeStruct((nb, n), F32),
        compiler_params=_cparams(("arbitrary",)),
    )(c_all, w_ada, b_ada.reshape(1, n))


def _inproj_kernel(x_ref, mod_ref, nw_ref, w_ref, wab_ref, cos_ref, sin_ref, o_ref, o1_ref, o2_ref, ab_ref,
                   h_scr, acc_scr):
    j = pl.program_id(1)
    tm = acc_scr.shape[1]
    nslab = PROJ_TN // HEAD_DIM

    @pl.when(j == 0)
    def _():
        x = x_ref[...]
        ms = jnp.mean(x * x, axis=-1, keepdims=True)
        y = x * lax.rsqrt(ms + NORM_EPS) * nw_ref[...]
        h = y * (1.0 + mod_ref[0, 1:2, :]) + mod_ref[0, 0:1, :]
        hb = h.astype(BF16)
        h_scr[...] = hb
        ab_ref[...] = _dot(hb, wab_ref[...])

    acc = _dot(h_scr[...], w_ref[...])
    is_rot = jnp.logical_and(j >= AT_TILE0, lax.rem(j - AT_TILE0 + 3, 3) != 2)
    is_main = j < MAIN_TILES

    def rotated(a):
        return a * cos_ref[...] + pltpu.roll(a, HEAD_DIM // 2, axis=1) * sin_ref[...]

    @pl.when(jnp.logical_and(is_main, jnp.logical_not(is_rot)))
    def _():
        o_ref[...] = acc.astype(BF16)

    @pl.when(jnp.logical_and(is_main, is_rot))
    def _():
        for hh in range(nslab):
            sl = slice(hh * HEAD_DIM, (hh + 1) * HEAD_DIM)
            o_ref[:, sl] = rotated(acc[:, sl]).astype(BF16)

    @pl.when(jnp.logical_and(jnp.logical_not(is_main), is_rot))
    def _():
        for hh in range(nslab):
            acc_scr[hh] = rotated(acc[:, hh * HEAD_DIM:(hh + 1) * HEAD_DIM])

    @pl.when(jnp.logical_and(jnp.logical_not(is_main), jnp.logical_not(is_rot)))
    def _():
        for hh in range(nslab):
            acc_scr[hh] = acc[:, hh * HEAD_DIM:(hh + 1) * HEAD_DIM]

    for o_dil, gi in ((o1_ref, 1), (o2_ref, 2)):
        dil = ATTN_GROUPS[gi][1]
        t0 = MAIN_TILES + 3 * (gi - 1)

        @pl.when(jnp.logical_and(j >= t0, j < t0 + 3))
        def _(o_dil=o_dil, dil=dil):
            for r in range(dil):
                for hh in range(nslab):
                    o_dil[0, r, :, hh * HEAD_DIM:(hh + 1) * HEAD_DIM] = (
                        acc_scr[hh, pl.ds(r, tm // dil, stride=dil), :].astype(BF16))


def _in_proj(x2, mod3, norm_w, w_main, w_ab, cos_t, sin_t, seq):
    rows = x2.shape[0]
    b = rows // seq
    tm = min(1024, seq)
    nt = seq // tm
    d1, d2 = ATTN_GROUPS[1][1], ATTN_GROUPS[2][1]

    def dil_spec(dil, t0):
        return pl.BlockSpec((1, dil, tm // dil, PROJ_TN),
                            lambda i, j: (i // nt, 0, i % nt, jnp.clip(j - t0, 0, 2)))

    return pl.pallas_call(
        _inproj_kernel,
        grid=(rows // tm, W_MAIN // PROJ_TN),
        in_specs=[pl.BlockSpec((tm, D_MODEL), lambda i, j: (i, 0)),
                  pl.BlockSpec((1, N_MOD, D_MODEL), lambda i, j: (i // nt, 0, 0)),
                  pl.BlockSpec((1, D_MODEL), lambda i, j: (0, 0)),
                  pl.BlockSpec((D_MODEL, PROJ_TN), lambda i, j: (0, j)),
                  pl.BlockSpec((D_MODEL, AB_COLS), lambda i, j: (0, 0)),
                  pl.BlockSpec((tm, HEAD_DIM), lambda i, j: (i % nt, 0)),
                  pl.BlockSpec((tm, HEAD_DIM), lambda i, j: (i % nt, 0))],
        out_specs=[pl.BlockSpec((tm, PROJ_TN), lambda i, j: (i, jnp.minimum(j, MAIN_TILES - 1))),
                   dil_spec(d1, MAIN_TILES), dil_spec(d2, MAIN_TILES + 3),
                   pl.BlockSpec((tm, AB_COLS), lambda i, j: (i, 0))],
        out_shape=[jax.ShapeDtypeStruct((rows, W_PROJ), BF16),
                   jax.ShapeDtypeStruct((b, d1, seq // d1, AT_GROUP_COLS), BF16),
                   jax.ShapeDtypeStruct((b, d2, seq // d2, AT_GROUP_COLS), BF16),
                   jax.ShapeDtypeStruct((rows, AB_COLS), F32)],
        scratch_shapes=[pltpu.VMEM((tm, D_MODEL), BF16), pltpu.VMEM((PROJ_TN // HEAD_DIM, tm, HEAD_DIM), F32)],
        compiler_params=_cparams(("parallel", "arbitrary")),
    )(x2, mod3, norm_w.reshape(1, D_MODEL), w_main, w_ab, cos_t, sin_t)


PREP_SEG = 512
PREP_NC = PREP_SEG // DN_CHUNK


def _tri_inverse(lm, eye):
    r = eye - lm
    p = lm
    for _ in range(5):
        pb = p.astype(BF16)
        p = _dot(pb, pb)
        r = r + _dot(r.astype(BF16), p.astype(BF16))
    return r


def _prep_direction(q, k, v, kk, qk, g_b, beta_b, incl, strict, tri_cat, tri_t, ones_cat, eye):
    c = DN_CHUNK
    g_hi_f = g_b.astype(BF16).astype(F32)
    g_lo_f = g_b - g_hi_f
    gc = _dot(tri_cat, jnp.concatenate([g_hi_f, g_lo_f], axis=0).astype(BF16))
    gr_rhs = jnp.concatenate([jnp.where(tri_t, g_hi_f[:, :c], 0.0), jnp.where(tri_t, g_lo_f[:, :c], 0.0)], axis=0)
    gr = _dot(ones_cat, gr_rhs.astype(BF16))
    gtot = jnp.sum(g_b, axis=0, keepdims=True)
    diff = gc[:, :c] - gr
    decay = jnp.where(incl, jnp.exp(jnp.where(incl, diff, 0.0)), 0.0)
    exp_g = jnp.exp(gc)
    lm = jnp.where(strict, beta_b[:, :c] * kk * decay, 0.0)
    tmat = _tri_inverse(lm, eye)
    kb = k * beta_b
    rhs = jnp.concatenate([kb * exp_g, v * beta_b], axis=1).astype(BF16)
    wu = _dot(tmat.astype(BF16), rhs).astype(BF16)
    qkm = jnp.where(incl, qk * decay, 0.0).astype(BF16)
    ao = _dot(qkm, wu)
    a_mat = q * exp_g - ao[:, :HEAD_DIM]
    o0 = ao[:, HEAD_DIM:]
    kd = (k * jnp.exp(gtot - gc)).astype(BF16)
    kb_mat = _dot_tn(kd, wu)
    return kb_mat[:, :HEAD_DIM], kb_mat[:, HEAD_DIM:], a_mat, o0, jnp.exp(gtot)


def _dn_prep_kernel(alog_ref, dtb_ref, main_ref, prev_ref, next_ref, ab_ref, cw_ref,
                    knf_ref, bnf_ref, af_ref, o0f_ref, glf_ref,
                    knb_ref, bnb_ref, ab_out_ref, o0b_ref, glb_ref, xe_scr):
    h = pl.program_id(1)
    s = pl.program_id(2)
    ns = pl.num_programs(2)
    c = DN_CHUNK
    seg = PREP_SEG

    pv = prev_ref[0].astype(F32)[8:16, :] * (s > 0).astype(F32)
    nx = next_ref[0].astype(F32)[0:8, :] * (s < ns - 1).astype(F32)
    xe_scr[0:8, :] = pv
    xe_scr[8:seg + 8, :] = main_ref[0].astype(F32)
    xe_scr[seg + 8:seg + 16, :] = nx

    row = lax.broadcasted_iota(jnp.int32, (c, c), 0)
    col = lax.broadcasted_iota(jnp.int32, (c, c), 1)
    row2 = lax.broadcasted_iota(jnp.int32, (c, 2 * c), 0)
    col2 = jnp.bitwise_and(lax.broadcasted_iota(jnp.int32, (c, 2 * c), 1), c - 1)
    eye = (row == col).astype(F32)
    ones_cat = jnp.ones((c, 2 * c), BF16)
    lane = lax.broadcasted_iota(jnp.int32, (c, HEAD_DIM), 1)
    dirs = (
        (row >= col, row > col, (col2 <= row2).astype(BF16), row <= col),
        (row <= col, row < col, (col2 >= row2).astype(BF16), row >= col),
    )
    alog = [jnp.full((c, 1), alog_ref[d, h], F32) for d in range(2)]
    dtb = [jnp.full((c, 1), dtb_ref[d, h], F32) for d in range(2)]
    cw = cw_ref[0]
    outs = ((knf_ref, bnf_ref, af_ref, o0f_ref, glf_ref), (knb_ref, bnb_ref, ab_out_ref, o0b_ref, glb_ref))

    def body(ci, carry):
        r0 = pl.multiple_of(ci * c, c)
        win = xe_scr[pl.ds(r0, c + 16), :]
        acc = cw[0:1, :] * win[6:6 + c, :]
        for t in range(1, DN_CONV):
            acc = acc + cw[t:t + 1, :] * win[6 + t:6 + t + c, :]
        x = _silu(acc)
        q = x[:, :HEAD_DIM]
        k = x[:, HEAD_DIM:2 * HEAD_DIM]
        v = x[:, 2 * HEAD_DIM:]
        q = q * lax.rsqrt(jnp.sum(q * q, axis=-1, keepdims=True) + NORM_EPS) * (HEAD_DIM ** -0.5)
        k = k * lax.rsqrt(jnp.sum(k * k, axis=-1, keepdims=True) + NORM_EPS)
        qb = q.astype(BF16)
        kbf = k.astype(BF16)
        kk = _dot_nt(kbf, kbf)
        qk = _dot_nt(qb, kbf)
        abc = ab_ref[0, pl.ds(r0, c), :]
        for d in range(2):
            incl, strict, tri_cat, tri_t = dirs[d]
            a_col = jnp.sum(jnp.where(lane == d * DN_HEADS + h, abc, 0.0), axis=-1, keepdims=True)
            b_col = jnp.sum(jnp.where(lane == 2 * DN_HEADS + d * DN_HEADS + h, abc, 0.0), axis=-1, keepdims=True)
            z = a_col + dtb[d]
            softplus = jnp.maximum(z, 0.0) + jnp.log(1.0 + jnp.exp(-jnp.abs(z)))
            g_col = -jnp.exp(alog[d]) * softplus
            beta_col = jax.nn.sigmoid(b_col)
            g_b = jnp.broadcast_to(g_col, (c, HEAD_DIM))
            beta_b = jnp.broadcast_to(beta_col, (c, HEAD_DIM))
            kn, bn, a_mat, o0, gl = _prep_direction(q, k, v, kk, qk, g_b, beta_b, incl, strict,
                                                    tri_cat, tri_t, ones_cat, eye)
            kn_ref, bn_ref, a_ref, o0_ref, gl_ref = outs[d]
            r2 = pl.multiple_of(ci * 2 * c, 2 * c)
            kn_ref[0, pl.ds(r2, 2 * c), :] = kn.astype(BF16)
            bn_ref[0, pl.ds(r2, 2 * c), :] = bn.astype(BF16)
            a_ref[0, pl.ds(r0, c), :] = a_mat.astype(BF16)
            o0_ref[0, pl.ds(r0, c), :] = o0.astype(BF16)
            gl_ref[0, pl.ds(pl.multiple_of(ci * 8, 8), 8), :] = jnp.broadcast_to(gl, (8, HEAD_DIM))
        return carry

    lax.fori_loop(0, PREP_NC, body, 0)


def _dn_prep(proj3, ab3, conv_h, a_log, dt_bias):
    b, seq, _ = proj3.shape
    seg = PREP_SEG
    ns = seq // seg
    hb16 = seg // 16
    last16 = seq // 16 - 1
    smem = pl.BlockSpec(memory_space=pltpu.SMEM)
    tok = lambda bi, h, s: (bi, s, h)
    big = jax.ShapeDtypeStruct((b, 2 * seq, DN_WIDTH), BF16)
    med = jax.ShapeDtypeStruct((b, seq, DN_WIDTH), BF16)
    gls = jax.ShapeDtypeStruct((b, seq // DN_CHUNK * 8, DN_WIDTH), F32)
    per_dir_specs = [pl.BlockSpec((1, 2 * seg, HEAD_DIM), tok), pl.BlockSpec((1, 2 * seg, HEAD_DIM), tok),
                     pl.BlockSpec((1, seg, HEAD_DIM), tok), pl.BlockSpec((1, seg, HEAD_DIM), tok),
                     pl.BlockSpec((1, PREP_NC * 8, HEAD_DIM), tok)]
    return pl.pallas_call(
        _dn_prep_kernel,
        grid=(b, DN_HEADS, ns),
        in_specs=[smem, smem,
                  pl.BlockSpec((1, seg, DN_HEAD_COLS), tok),
                  pl.BlockSpec((1, 16, DN_HEAD_COLS), lambda bi, h, s: (bi, jnp.maximum(s * hb16 - 1, 0), h)),
                  pl.BlockSpec((1, 16, DN_HEAD_COLS), lambda bi, h, s: (bi, jnp.minimum((s + 1) * hb16, last16), h)),
                  pl.BlockSpec((1, seg, AB_COLS), lambda bi, h, s: (bi, s, 0)),
                  pl.BlockSpec((1, 8, DN_HEAD_COLS), lambda bi, h, s: (h, 0, 0))],
        out_specs=per_dir_specs + per_dir_specs,
        out_shape=[big, big, med, med, gls] * 2,
        scratch_shapes=[pltpu.VMEM((seg + 16, DN_HEAD_COLS), F32)],
        compiler_params=_cparams(("parallel", "parallel", "parallel")),
    )(a_log, dt_bias, proj3, proj3, proj3, ab3, conv_h)


SCAN_HB = 4


def _dn_scan_kernel(knf_ref, bnf_ref, af_ref, o0f_ref, glf_ref, knb_ref, bnb_ref, ab_ref, o0b_ref, glb_ref,
                    of_ref, ob_ref, s_scr, *, nc):
    c = DN_CHUNK

    @pl.when(pl.program_id(2) == 0)
    def _():
        s_scr[...] = jnp.zeros(s_scr.shape, F32)

    def chain(idx, ci, kn_ref, bn_ref, a_ref, o0_ref, gl_ref, o_ref, hh):
        sl = slice(hh * HEAD_DIM, (hh + 1) * HEAD_DIM)
        r2 = pl.multiple_of(ci * 2 * c, 2 * c)
        r1 = pl.multiple_of(ci * c, c)
        lhs = jnp.concatenate([kn_ref[0, pl.ds(r2, 2 * c), sl], a_ref[0, pl.ds(r1, c), sl]], axis=0)
        st = s_scr[idx]
        t1 = _dot(lhs, st.astype(BF16))
        gl = gl_ref[0, pl.ds(pl.multiple_of(ci * 8, 8), 8), sl][0:1, :]
        s_scr[idx] = gl * st - t1[:2 * c] + bn_ref[0, pl.ds(r2, 2 * c), sl].astype(F32)
        o_ref[0, pl.ds(r1, c), sl] = (t1[2 * c:] + o0_ref[0, pl.ds(r1, c), sl].astype(F32)).astype(o_ref.dtype)

    def body(ci, carry):
        cb = nc - 1 - ci
        for hh in range(SCAN_HB):
            chain(hh, ci, knf_ref, bnf_ref, af_ref, o0f_ref, glf_ref, of_ref, hh)
            chain(SCAN_HB + hh, cb, knb_ref, bnb_ref, ab_ref, o0b_ref, glb_ref, ob_ref, hh)
        return carry

    lax.fori_loop(0, nc, body, 0)


def _dn_scan(prep):
    knf, bnf, af, o0f, glf, knb, bnb, ab_, o0b, glb = prep
    b, seq, _ = af.shape
    seg = min(1024, seq)
    nc = seg // DN_CHUNK
    ns = seq // seg
    w = SCAN_HB * HEAD_DIM
    fwd = lambda bi, hb, s: (bi, s, hb)
    bwd = lambda bi, hb, s: (bi, ns - 1 - s, hb)

    def specs(im):
        return [pl.BlockSpec((1, 2 * seg, w), im), pl.BlockSpec((1, 2 * seg, w), im),
                pl.BlockSpec((1, seg, w), im), pl.BlockSpec((1, seg, w), im),
                pl.BlockSpec((1, nc * 8, w), im)]

    out = jax.ShapeDtypeStruct((b, seq, DN_WIDTH), BF16)
    return pl.pallas_call(
        functools.partial(_dn_scan_kernel, nc=nc),
        grid=(b, DN_HEADS // SCAN_HB, ns),
        in_specs=specs(fwd) + specs(bwd),
        out_specs=[pl.BlockSpec((1, seg, w), fwd), pl.BlockSpec((1, seg, w), bwd)],
        out_shape=[out, out],
        scratch_shapes=[pltpu.VMEM((2 * SCAN_HB, HEAD_DIM, HEAD_DIM), F32)],
        compiler_params=_cparams(("parallel", "parallel", "arbitrary")),
    )(knf, bnf, af, o0f, glf, knb, bnb, ab_, o0b, glb)


def _attn_kernel(q_ref, kp_ref, km_ref, kn_ref, vp_ref, vm_ref, vn_ref, o_ref, lse_ref, *, tq):
    i = pl.program_id(2)
    nq = pl.num_programs(2)
    scale = HEAD_DIM ** -0.5
    row_m = lax.broadcasted_iota(jnp.int32, (tq, tq), 0)
    col_m = lax.broadcasted_iota(jnp.int32, (tq, tq), 1)
    mask_m = jnp.abs(row_m - col_m) <= HALO
    row_h = lax.broadcasted_iota(jnp.int32, (tq, HALO), 0)
    col_h = lax.broadcasted_iota(jnp.int32, (tq, HALO), 1)
    mask_p = jnp.logical_and(row_h <= col_h, i > 0)
    mask_n = jnp.logical_and(row_h - col_h >= tq - HALO, i < nq - 1)
    lane = lax.broadcasted_iota(jnp.int32, (tq, HEAD_DIM), 1)
    lse_tile = jnp.zeros((tq, HEAD_DIM), F32)
    for hh in range(HPG):
        sl = slice(hh * HEAD_DIM, (hh + 1) * HEAD_DIM)
        q = q_ref[0, 0, :, sl]
        sp = jnp.where(mask_p, _dot_nt(q, kp_ref[0, 0, :, sl]) * scale, MASK_VALUE)
        sm = jnp.where(mask_m, _dot_nt(q, km_ref[0, 0, :, sl]) * scale, MASK_VALUE)
        sn = jnp.where(mask_n, _dot_nt(q, kn_ref[0, 0, :, sl]) * scale, MASK_VALUE)
        m = jnp.maximum(jnp.max(sm, axis=-1, keepdims=True),
                        jnp.maximum(jnp.max(sp, axis=-1, keepdims=True), jnp.max(sn, axis=-1, keepdims=True)))
        pp = jnp.exp(sp - m)
        pm = jnp.exp(sm - m)
        pn = jnp.exp(sn - m)
        den = (jnp.sum(pm, axis=-1, keepdims=True) + jnp.sum(pp, axis=-1, keepdims=True)
               + jnp.sum(pn, axis=-1, keepdims=True))
        o = (_dot(pm.astype(BF16), vm_ref[0, 0, :, sl]) + _dot(pp.astype(BF16), vp_ref[0, 0, :, sl])
             + _dot(pn.astype(BF16), vn_ref[0, 0, :, sl]))
        o_ref[0, 0, :, sl] = (o / den).astype(o_ref.dtype)
        lse_tile = jnp.where(lane == hh, m + jnp.log(den), lse_tile)
    lse_ref[0, 0] = lse_tile


def _attention_group(at4, cq):
    b, dil, m, _ = at4.shape
    tq = min(256, m)
    nq = m // tq
    hq = tq // HALO
    lastb = m // HALO - 1

    def main(off):
        return pl.BlockSpec((1, 1, tq, ATTN_OUT), lambda bi, r, i: (bi, r, i, cq + off))

    def prev(off):
        return pl.BlockSpec((1, 1, HALO, ATTN_OUT), lambda bi, r, i: (bi, r, jnp.maximum(i * hq - 1, 0), cq + off))

    def nxt(off):
        return pl.BlockSpec((1, 1, HALO, ATTN_OUT),
                            lambda bi, r, i: (bi, r, jnp.minimum((i + 1) * hq, lastb), cq + off))

    return pl.pallas_call(
        functools.partial(_attn_kernel, tq=tq),
        grid=(b, dil, nq),
        in_specs=[main(0), prev(1), main(1), nxt(1), prev(2), main(2), nxt(2)],
        out_specs=[pl.BlockSpec((1, 1, tq, ATTN_OUT), lambda bi, r, i: (bi, r, i, 0)),
                   pl.BlockSpec((1, 1, tq, HEAD_DIM), lambda bi, r, i: (bi, r, i, 0))],
        out_shape=[jax.ShapeDtypeStruct((b, dil, m, ATTN_OUT), BF16),
                   jax.ShapeDtypeStruct((b, dil, m, HEAD_DIM), F32)],
        compiler_params=_cparams(("parallel", "parallel", "parallel")),
    )(at4, at4, at4, at4, at4, at4, at4)


MIX_TM = 256


def _mix_kernel(x_ref, mod_ref, of_ref, ob_ref, z_ref, gdn_ref, gat_ref,
                o1_ref, o2_ref, o3_ref, l1_ref, l2_ref, l3_ref,
                dnw_ref, wdn_ref, wat_ref, wout_ref, npost_ref, out_ref, a_scr, b_scr,
                o2_scr, o3_scr, l2_scr, l3_scr):
    tm = a_scr.shape[0]
    for src, lsrc, dst, ldst, gi in ((o2_ref, l2_ref, o2_scr, l2_scr, 1), (o3_ref, l3_ref, o3_scr, l3_scr, 2)):
        dil = ATTN_GROUPS[gi][1]
        for r in range(dil):
            for hh in range(HPG):
                dst[hh, pl.ds(r, tm // dil, stride=dil), :] = (
                    src[0, r, :, hh * HEAD_DIM:(hh + 1) * HEAD_DIM].astype(F32))
            ldst[pl.ds(r, tm // dil, stride=dil), :] = lsrc[0, r]
    dnw = dnw_ref[...]
    for h in range(DN_HEADS):
        sl = slice(h * HEAD_DIM, (h + 1) * HEAD_DIM)
        o = of_ref[:, sl].astype(F32) + ob_ref[:, sl].astype(F32)
        y = o * lax.rsqrt(jnp.mean(o * o, axis=-1, keepdims=True) + NORM_EPS) * dnw
        a_scr[:, sl] = (y * _silu(z_ref[:, sl].astype(F32))).astype(BF16)
    l1 = l1_ref[...]
    l2 = l2_scr[...]
    l3 = l3_scr[...]
    lm = jnp.maximum(l1, jnp.maximum(l2, l3))
    e1 = jnp.exp(l1 - lm)
    e2 = jnp.exp(l2 - lm)
    e3 = jnp.exp(l3 - lm)
    inv = 1.0 / (e1 + e2 + e3)
    w1 = e1 * inv
    w2 = e2 * inv
    w3 = e3 * inv
    for h in range(HPG):
        sl = slice(h * HEAD_DIM, (h + 1) * HEAD_DIM)
        b_scr[:, sl] = (w1[:, h:h + 1] * o1_ref[:, sl].astype(F32) + w2[:, h:h + 1] * o2_scr[h]
                        + w3[:, h:h + 1] * o3_scr[h]).astype(BF16)
    y_dn = _dot(a_scr[...], wdn_ref[...])
    y_at = _dot(b_scr[...], wat_ref[...])
    mixed_in = (jax.nn.sigmoid(gdn_ref[...].astype(F32)) * y_dn
                + jax.nn.sigmoid(gat_ref[...].astype(F32)) * y_at).astype(BF16)
    mixed = _dot(mixed_in, wout_ref[...])
    nrm = mixed * lax.rsqrt(jnp.mean(mixed * mixed, axis=-1, keepdims=True) + NORM_EPS) * npost_ref[...]
    out_ref[...] = x_ref[...] + mod_ref[0, 2:3, :] * nrm


def _mix(x2, mod3, proj2, o_f, o_b, at_o, at_lse, dn_norm_w, w_dn_out, w_at_out, w_out, norm_post, seq):
    rows = x2.shape[0]
    tm = MIX_TM
    nt = seq // tm
    rowblk = lambda w: pl.BlockSpec((tm, w), lambda i: (i, 0))
    const = lambda shape: pl.BlockSpec(shape, lambda i: (0, 0))
    d1, d2 = ATTN_GROUPS[1][1], ATTN_GROUPS[2][1]
    dilblk = lambda dil, w: pl.BlockSpec((1, dil, tm // dil, w), lambda i: (i // nt, 0, i % nt, 0))
    return pl.pallas_call(
        _mix_kernel,
        grid=(rows // tm,),
        in_specs=[rowblk(D_MODEL),
                  pl.BlockSpec((1, N_MOD, D_MODEL), lambda i: (i // nt, 0, 0)),
                  rowblk(DN_WIDTH), rowblk(DN_WIDTH),
                  pl.BlockSpec((tm, DN_WIDTH), lambda i: (i, COL_Z // DN_WIDTH)),
                  pl.BlockSpec((tm, D_MODEL), lambda i: (i, COL_MERGE // D_MODEL)),
                  pl.BlockSpec((tm, D_MODEL), lambda i: (i, COL_MERGE // D_MODEL + 1)),
                  rowblk(ATTN_OUT), dilblk(d1, ATTN_OUT), dilblk(d2, ATTN_OUT),
                  rowblk(HEAD_DIM), dilblk(d1, HEAD_DIM), dilblk(d2, HEAD_DIM),
                  const((1, HEAD_DIM)), const((DN_WIDTH, D_MODEL)), const((ATTN_OUT, D_MODEL)),
                  const((D_MODEL, D_MODEL)), const((1, D_MODEL))],
        out_specs=rowblk(D_MODEL),
        out_shape=jax.ShapeDtypeStruct((rows, D_MODEL), F32),
        scratch_shapes=[pltpu.VMEM((tm, DN_WIDTH), BF16), pltpu.VMEM((tm, ATTN_OUT), BF16),
                        pltpu.VMEM((HPG, tm, HEAD_DIM), F32), pltpu.VMEM((HPG, tm, HEAD_DIM), F32),
                        pltpu.VMEM((tm, HEAD_DIM), F32), pltpu.VMEM((tm, HEAD_DIM), F32)],
        compiler_params=_cparams(("parallel",)),
    )(x2, mod3, o_f, o_b, proj2, proj2, proj2, at_o[0].reshape(rows, ATTN_OUT), at_o[1], at_o[2],
      at_lse[0].reshape(rows, HEAD_DIM), at_lse[1], at_lse[2],
      dn_norm_w.reshape(1, HEAD_DIM), w_dn_out, w_at_out, w_out, norm_post.reshape(1, D_MODEL))


FFN_TM = 512
FFN_TF = 512


def _ffn_kernel(x_ref, mod_ref, npre_ref, w1_ref, w2_ref, npost_ref, out_ref, h_scr, acc_scr):
    j = pl.program_id(1)

    @pl.when(j == 0)
    def _():
        x = x_ref[...]
        y = x * lax.rsqrt(jnp.mean(x * x, axis=-1, keepdims=True) + NORM_EPS) * npre_ref[...]
        h_scr[...] = (y * (1.0 + mod_ref[0, 4:5, :]) + mod_ref[0, 3:4, :]).astype(BF16)
        acc_scr[...] = jnp.zeros(acc_scr.shape, F32)

    t = jnp.maximum(_dot(h_scr[...], w1_ref[...]), 0.0)
    acc_scr[...] += _dot((t * t).astype(BF16), w2_ref[...])

    @pl.when(j == pl.num_programs(1) - 1)
    def _():
        f = acc_scr[...]
        nrm = f * lax.rsqrt(jnp.mean(f * f, axis=-1, keepdims=True) + NORM_EPS) * npost_ref[...]
        out_ref[...] = x_ref[...] + mod_ref[0, 5:6, :] * nrm


def _ffn(x2, mod3, norm_pre, w_ff1, w_ff2, norm_post, seq):
    rows = x2.shape[0]
    tm = FFN_TM
    nt = seq // tm
    return pl.pallas_call(
        _ffn_kernel,
        grid=(rows // tm, D_FF // FFN_TF),
        in_specs=[pl.BlockSpec((tm, D_MODEL), lambda i, j: (i, 0)),
                  pl.BlockSpec((1, N_MOD, D_MODEL), lambda i, j: (i // nt, 0, 0)),
                  pl.BlockSpec((1, D_MODEL), lambda i, j: (0, 0)),
                  pl.BlockSpec((D_MODEL, FFN_TF), lambda i, j: (0, j)),
                  pl.BlockSpec((FFN_TF, D_MODEL), lambda i, j: (j, 0)),
                  pl.BlockSpec((1, D_MODEL), lambda i, j: (0, 0))],
        out_specs=pl.BlockSpec((tm, D_MODEL), lambda i, j: (i, 0)),
        out_shape=jax.ShapeDtypeStruct((rows, D_MODEL), F32),
        scratch_shapes=[pltpu.VMEM((tm, D_MODEL), BF16), pltpu.VMEM((tm, D_MODEL), F32)],
        compiler_params=_cparams(("parallel", "arbitrary")),
    )(x2, mod3, norm_pre.reshape(1, D_MODEL), w_ff1, w_ff2, norm_post.reshape(1, D_MODEL))


def _prep_in_weights(w_in, conv_w):
    o = 0
    dn_qkv = w_in[:, o:o + 3 * DN_WIDTH]; o += 3 * DN_WIDTH
    dn_z = w_in[:, o:o + DN_WIDTH]; o += DN_WIDTH
    dn_ab = w_in[:, o:o + 4 * DN_HEADS]; o += 4 * DN_HEADS
    at_qkv = w_in[:, o:o + 3 * ATTN_WIDTH]; o += 3 * ATTN_WIDTH
    merge = w_in[:, o:o + 2 * D_MODEL]
    dn_ph = dn_qkv.reshape(D_MODEL, 3, DN_HEADS, HEAD_DIM).transpose(0, 2, 1, 3).reshape(D_MODEL, 3 * DN_WIDTH)
    at_pg = (at_qkv.reshape(D_MODEL, 3, N_GROUPS, ATTN_OUT).transpose(0, 2, 1, 3)
             .reshape(D_MODEL, 3 * ATTN_WIDTH))
    w_main = jnp.concatenate([dn_ph, dn_z, merge, at_pg], axis=1).astype(BF16)
    w_ab = jnp.pad(dn_ab, ((0, 0), (0, AB_COLS - 4 * DN_HEADS))).astype(BF16)
    conv_h = conv_w.reshape(DN_CONV, 3, DN_HEADS, HEAD_DIM).transpose(2, 0, 1, 3).reshape(DN_HEADS, DN_CONV, DN_HEAD_COLS)
    conv_h = jnp.pad(conv_h, ((0, 0), (0, 8 - DN_CONV), (0, 0)))
    return w_main, w_ab, conv_h


def _rope_tables(seq):
    half = HEAD_DIM // 2
    inv_freq = ROPE_THETA ** (-jnp.arange(half, dtype=F32) / half)
    ang = jnp.arange(seq, dtype=F32)[:, None] * inv_freq[None, :]
    cos = jnp.cos(ang)
    sin = jnp.sin(ang)
    return jnp.concatenate([cos, cos], axis=1), jnp.concatenate([-sin, sin], axis=1)


def _group_forward(x, mod, wts):
    b, seq, _ = x.shape
    rows = b * seq
    x2 = x.reshape(rows, D_MODEL)
    mod3 = mod.reshape(b, N_MOD, D_MODEL)
    cos_t, sin_t = _rope_tables(seq)
    proj2, at1, at2, ab2 = _in_proj(x2, mod3, wts["norm_pre_mix"], wts["w_main"], wts["w_ab"], cos_t, sin_t, seq)
    proj3 = proj2.reshape(b, seq, W_PROJ)
    prep = _dn_prep(proj3, ab2.reshape(b, seq, AB_COLS), wts["conv_h"], wts["A_log"], wts["dt_bias"])
    o_f, o_b = _dn_scan(prep)
    at = [_attention_group(proj2.reshape(b, 1, seq, W_PROJ), COL_AT // ATTN_OUT),
          _attention_group(at1, 0), _attention_group(at2, 0)]
    x1 = _mix(x2, mod3, proj2, o_f.reshape(rows, DN_WIDTH), o_b.reshape(rows, DN_WIDTH),
              [a[0] for a in at], [a[1] for a in at], wts["dn_norm_w"], wts["w_dn_out"], wts["w_at_out"],
              wts["w_out"], wts["norm_post_mix"], seq)
    y = _ffn(x1, mod3, wts["norm_pre_ffn"], wts["w_ff1"], wts["w_ff2"], wts["norm_post_ffn"], seq)
    return y.reshape(b, seq, D_MODEL)


def kernel(x_prompt, x_sample, c_prompt, c_sample, w_ada, b_ada, norm_pre_mix, norm_post_mix, norm_pre_ffn,
           norm_post_ffn, w_in, conv_w, A_log, dt_bias, dn_norm_w, w_dn_out, w_at_out, w_out, w_ff1, w_ff2):
    xs = (x_prompt, x_sample)
    nbp = c_prompt.shape[0]
    c_all = jnp.concatenate([c_prompt, c_sample], axis=0)
    for l in range(w_ada.shape[0]):
        w_main, w_ab, conv_h = _prep_in_weights(w_in[l], conv_w[l])
        wts = dict(norm_pre_mix=norm_pre_mix[l], norm_post_mix=norm_post_mix[l], norm_pre_ffn=norm_pre_ffn[l],
                   norm_post_ffn=norm_post_ffn[l], w_main=w_main, w_ab=w_ab, conv_h=conv_h, A_log=A_log[l],
                   dt_bias=dt_bias[l], dn_norm_w=dn_norm_w[l], w_dn_out=w_dn_out[l].astype(BF16),
                   w_at_out=w_at_out[l].astype(BF16), w_out=w_out[l].astype(BF16),
                   w_ff1=w_ff1[l].astype(BF16), w_ff2=w_ff2[l].astype(BF16))
        mod_all = _ada_mod(c_all, w_ada[l], b_ada[l])
        xs = (_group_forward(xs[0], mod_all[:nbp], wts), _group_forward(xs[1], mod_all[nbp:], wts))
    return xs
```

```python
import functools

import jax
import jax.numpy as jnp
from jax import lax
from jax.experimental import pallas as pl
from jax.experimental.pallas import tpu as pltpu

F32 = jnp.float32
BF16 = jnp.bfloat16

D_MODEL = 2048
HEAD_DIM = 128
DN_HEADS = 8
DN_WIDTH = DN_HEADS * HEAD_DIM
DN_CONV = 5
DN_CHUNK = 64
ATTN_GROUPS = ((128, 1), (512, 4), (2048, 16))
N_GROUPS = 3
HPG = 4
ATTN_HEADS = N_GROUPS * HPG
ATTN_WIDTH = ATTN_HEADS * HEAD_DIM
ATTN_OUT = HPG * HEAD_DIM
ROPE_THETA = 10000.0
D_FF = 4 * D_MODEL
N_MOD = 6
NORM_EPS = 1e-6
MASK_VALUE = -1e30

DN_HEAD_COLS = 3 * HEAD_DIM
COL_DN = 0
COL_Z = COL_DN + 3 * DN_WIDTH
COL_MERGE = COL_Z + DN_WIDTH
COL_AT = COL_MERGE + 2 * D_MODEL
AT_GROUP_COLS = 3 * ATTN_OUT
W_MAIN = COL_AT + 3 * ATTN_WIDTH
AB_COLS = 128

W_PROJ = COL_AT + AT_GROUP_COLS

PROJ_TN = 512
AT_TILE0 = COL_AT // PROJ_TN
MAIN_TILES = W_PROJ // PROJ_TN
HALO = 64

VMEM_LIMIT = 56 * 1024 * 1024


def _cparams(sem):
    return pltpu.CompilerParams(dimension_semantics=sem, vmem_limit_bytes=VMEM_LIMIT)


def _dot(a, b):
    return jnp.dot(a, b, preferred_element_type=F32)


def _dot_nt(a, b):
    return lax.dot_general(a, b, (((1,), (1,)), ((), ())), preferred_element_type=F32)


def _dot_tn(a, b):
    return lax.dot_general(a, b, (((0,), (0,)), ((), ())), preferred_element_type=F32)


def _silu(x):
    return x * jax.nn.sigmoid(x)


def _mod_kernel(c_ref, w_ref, b_ref, o_ref):
    s = _silu(c_ref[...]).astype(BF16)
    o_ref[...] = _dot(s, w_ref[...].astype(BF16)) + b_ref[...]


def _ada_mod(c_all, w_ada, b_ada):
    nb, n = c_all.shape[0], w_ada.shape[1]
    tn = 512
    return pl.pallas_call(
        _mod_kernel,
        grid=(n // tn,),
        in_specs=[pl.BlockSpec((nb, D_MODEL), lambda j: (0, 0)),
                  pl.BlockSpec((D_MODEL, tn), lambda j: (0, j)),
                  pl.BlockSpec((1, tn), lambda j: (0, j))],
        out_specs=pl.BlockSpec((nb, tn), lambda j: (0, j)),
        out_shape=jax.ShapeDtypeStruct((nb, n), F32),
        compiler_params=_cparams(("arbitrary",)),
    )(c_all, w_ada, b_ada.reshape(1, n))


def _inproj_kernel(x_ref, mod_ref, nw_ref, w_ref, wab_ref, cos_ref, sin_ref, o_ref, o1_ref, o2_ref, ab_ref,
                   h_scr, acc_scr):
    j = pl.program_id(1)
    tm = acc_scr.shape[1]
    nslab = PROJ_TN // HEAD_DIM

    @pl.when(j == 0)
    def _():
        x = x_ref[...]
        ms = jnp.mean(x * x, axis=-1, keepdims=True)
        y = x * lax.rsqrt(ms + NORM_EPS) * nw_ref[...]
        h = y * (1.0 + mod_ref[0, 1:2, :]) + mod_ref[0, 0:1, :]
        hb = h.astype(BF16)
        h_scr[...] = hb
        ab_ref[...] = _dot(hb, wab_ref[...])

    acc = _dot(h_scr[...], w_ref[...])
    is_rot = jnp.logical_and(j >= AT_TILE0, lax.rem(j - AT_TILE0 + 3, 3) != 2)
    is_main = j < MAIN_TILES

    def rotated(a):
        return a * cos_ref[...] + pltpu.roll(a, HEAD_DIM // 2, axis=1) * sin_ref[...]

    @pl.when(jnp.logical_and(is_main, jnp.logical_not(is_rot)))
    def _():
        o_ref[...] = acc.astype(BF16)

    @pl.when(jnp.logical_and(is_main, is_rot))
    def _():
        for hh in range(nslab):
            sl = slice(hh * HEAD_DIM, (hh + 1) * HEAD_DIM)
            o_ref[:, sl] = rotated(acc[:, sl]).astype(BF16)

    @pl.when(jnp.logical_and(jnp.logical_not(is_main), is_rot))
    def _():
        for hh in range(nslab):
            acc_scr[hh] = rotated(acc[:, hh * HEAD_DIM:(hh + 1) * HEAD_DIM])

    @pl.when(jnp.logical_and(jnp.logical_not(is_main), jnp.logical_not(is_rot)))
    def _():
        for hh in range(nslab):
            acc_scr[hh] = acc[:, hh * HEAD_DIM:(hh + 1) * HEAD_DIM]

    for o_dil, gi in ((o1_ref, 1), (o2_ref, 2)):
        dil = ATTN_GROUPS[gi][1]
        t0 = MAIN_TILES + 3 * (gi - 1)

        @pl.when(jnp.logical_and(j >= t0, j < t0 + 3))
        def _(o_dil=o_dil, dil=dil):
            for r in range(dil):
                for hh in range(nslab):
                    o_dil[0, r, :, hh * HEAD_DIM:(hh + 1) * HEAD_DIM] = (
                        acc_scr[hh, pl.ds(r, tm // dil, stride=dil), :].astype(BF16))


def _in_proj(x2, mod3, norm_w, w_main, w_ab, cos_t, sin_t, seq):
    rows = x2.shape[0]
    b = rows // seq
    tm = min(1024, seq)
    nt = seq // tm
    d1, d2 = ATTN_GROUPS[1][1], ATTN_GROUPS[2][1]

    def dil_spec(dil, t0):
        return pl.BlockSpec((1, dil, tm // dil, PROJ_TN),
                            lambda i, j: (i // nt, 0, i % nt, jnp.clip(j - t0, 0, 2)))

    return pl.pallas_call(
        _inproj_kernel,
        grid=(rows // tm, W_MAIN // PROJ_TN),
        in_specs=[pl.BlockSpec((tm, D_MODEL), lambda i, j: (i, 0)),
                  pl.BlockSpec((1, N_MOD, D_MODEL), lambda i, j: (i // nt, 0, 0)),
                  pl.BlockSpec((1, D_MODEL), lambda i, j: (0, 0)),
                  pl.BlockSpec((D_MODEL, PROJ_TN), lambda i, j: (0, j)),
                  pl.BlockSpec((D_MODEL, AB_COLS), lambda i, j: (0, 0)),
                  pl.BlockSpec((tm, HEAD_DIM), lambda i, j: (i % nt, 0)),
                  pl.BlockSpec((tm, HEAD_DIM), lambda i, j: (i % nt, 0))],
        out_specs=[pl.BlockSpec((tm, PROJ_TN), lambda i, j: (i, jnp.minimum(j, MAIN_TILES - 1))),
                   dil_spec(d1, MAIN_TILES), dil_spec(d2, MAIN_TILES + 3),
                   pl.BlockSpec((tm, AB_COLS), lambda i, j: (i, 0))],
        out_shape=[jax.ShapeDtypeStruct((rows, W_PROJ), BF16),
                   jax.ShapeDtypeStruct((b, d1, seq // d1, AT_GROUP_COLS), BF16),
                   jax.ShapeDtypeStruct((b, d2, seq // d2, AT_GROUP_COLS), BF16),
                   jax.ShapeDtypeStruct((rows, AB_COLS), F32)],
        scratch_shapes=[pltpu.VMEM((tm, D_MODEL), BF16), pltpu.VMEM((PROJ_TN // HEAD_DIM, tm, HEAD_DIM), F32)],
        compiler_params=_cparams(("parallel", "arbitrary")),
    )(x2, mod3, norm_w.reshape(1, D_MODEL), w_main, w_ab, cos_t, sin_t)


PREP_SEG = 512
PREP_NC = PREP_SEG // DN_CHUNK
PREP_UNROLL = 4


def _each(f, *lists):
    return [f(*args) for args in zip(*lists)]


def _tri_inverse_each(lms, eye):
    rs = _each(lambda lm: eye - lm, lms)
    ps = lms
    for _ in range(5):
        pbs = _each(lambda p: p.astype(BF16), ps)
        ps = _each(lambda pb: _dot(pb, pb), pbs)
        rs = _each(lambda r, p: r + _dot(r.astype(BF16), p.astype(BF16)), rs, ps)
    return rs


def _dn_prep_kernel(alog_ref, dtb_ref, main_ref, prev_ref, next_ref, ab_ref, cw_ref,
                    knf_ref, bnf_ref, af_ref, o0f_ref, glf_ref,
                    knb_ref, bnb_ref, ab_out_ref, o0b_ref, glb_ref, xe_scr):
    h = pl.program_id(1)
    s = pl.program_id(2)
    ns = pl.num_programs(2)
    c = DN_CHUNK
    seg = PREP_SEG

    pv = prev_ref[0].astype(F32)[8:16, :] * (s > 0).astype(F32)
    nx = next_ref[0].astype(F32)[0:8, :] * (s < ns - 1).astype(F32)
    xe_scr[0:8, :] = pv
    xe_scr[8:seg + 8, :] = main_ref[0].astype(F32)
    xe_scr[seg + 8:seg + 16, :] = nx

    row = lax.broadcasted_iota(jnp.int32, (c, c), 0)
    col = lax.broadcasted_iota(jnp.int32, (c, c), 1)
    row2 = lax.broadcasted_iota(jnp.int32, (c, 2 * c), 0)
    col2 = jnp.bitwise_and(lax.broadcasted_iota(jnp.int32, (c, 2 * c), 1), c - 1)
    eye = (row == col).astype(F32)
    ones_cat = jnp.ones((c, 2 * c), BF16)
    lane = lax.broadcasted_iota(jnp.int32, (c, HEAD_DIM), 1)
    dirs = (
        (row >= col, row > col, (col2 <= row2).astype(BF16), row <= col),
        (row <= col, row < col, (col2 >= row2).astype(BF16), row >= col),
    )
    alog = [jnp.full((c, 1), alog_ref[d, h], F32) for d in range(2)]
    dtb = [jnp.full((c, 1), dtb_ref[d, h], F32) for d in range(2)]
    cw = cw_ref[0]
    outs = ((knf_ref, bnf_ref, af_ref, o0f_ref, glf_ref), (knb_ref, bnb_ref, ab_out_ref, o0b_ref, glb_ref))

    nb = PREP_UNROLL
    probs = [(j, d) for j in range(nb) for d in range(2)]
    pj = [j for j, _ in probs]
    pd = [d for _, d in probs]
    incl = [dirs[d][0] for d in pd]
    strict = [dirs[d][1] for d in pd]
    tri_cat = [dirs[d][2] for d in pd]
    tri_t = [dirs[d][3] for d in pd]

    def body(it, carry):
        ci = [it * nb + j for j in range(nb)]
        r0 = [pl.multiple_of(cj * c, c) for cj in ci]
        xs = []
        for j in range(nb):
            win = xe_scr[pl.ds(r0[j], c + 16), :]
            acc = cw[0:1, :] * win[6:6 + c, :]
            for t in range(1, DN_CONV):
                acc = acc + cw[t:t + 1, :] * win[6 + t:6 + t + c, :]
            xs.append(_silu(acc))
        q = [x[:, :HEAD_DIM] for x in xs]
        k = [x[:, HEAD_DIM:2 * HEAD_DIM] for x in xs]
        v = [x[:, 2 * HEAD_DIM:] for x in xs]
        q = _each(lambda t: t * lax.rsqrt(jnp.sum(t * t, axis=-1, keepdims=True) + NORM_EPS) * (HEAD_DIM ** -0.5), q)
        k = _each(lambda t: t * lax.rsqrt(jnp.sum(t * t, axis=-1, keepdims=True) + NORM_EPS), k)
        qb = _each(lambda t: t.astype(BF16), q)
        kbf = _each(lambda t: t.astype(BF16), k)
        kk = _each(_dot_nt, kbf, kbf)
        qk = _each(_dot_nt, qb, kbf)
        abc = [ab_ref[0, pl.ds(r0[j], c), :] for j in range(nb)]

        def gate_cols(j, d):
            a_col = jnp.sum(jnp.where(lane == d * DN_HEADS + h, abc[j], 0.0), axis=-1, keepdims=True)
            b_col = jnp.sum(jnp.where(lane == 2 * DN_HEADS + d * DN_HEADS + h, abc[j], 0.0), axis=-1, keepdims=True)
            z = a_col + dtb[d]
            softplus = jnp.maximum(z, 0.0) + jnp.log(1.0 + jnp.exp(-jnp.abs(z)))
            g_col = -jnp.exp(alog[d]) * softplus
            return jnp.broadcast_to(g_col, (c, HEAD_DIM)), jnp.broadcast_to(jax.nn.sigmoid(b_col), (c, HEAD_DIM))

        gates = _each(gate_cols, pj, pd)
        g_b = [g for g, _ in gates]
        beta_b = [bt for _, bt in gates]
        g_hi = _each(lambda g: g.astype(BF16).astype(F32), g_b)
        g_lo = _each(lambda g, gh: g - gh, g_b, g_hi)
        gc = _each(lambda t, gh, gl_: _dot(t, jnp.concatenate([gh, gl_], axis=0).astype(BF16)), tri_cat, g_hi, g_lo)
        gr = _each(lambda tt, gh, gl_: _dot(ones_cat, jnp.concatenate(
            [jnp.where(tt, gh[:, :c], 0.0), jnp.where(tt, gl_[:, :c], 0.0)], axis=0).astype(BF16)), tri_t, g_hi, g_lo)
        gtot = _each(lambda g: jnp.sum(g, axis=0, keepdims=True), g_b)
        decay = _each(lambda m, a, b_: jnp.where(m, jnp.exp(jnp.where(m, a[:, :c] - b_, 0.0)), 0.0), incl, gc, gr)
        exp_g = _each(jnp.exp, gc)
        lm = _each(lambda m, bt, j, dc: jnp.where(m, bt[:, :c] * kk[j] * dc, 0.0), strict, beta_b, pj, decay)
        tmat = _tri_inverse_each(lm, eye)
        rhs = _each(lambda j, bt, eg: jnp.concatenate([k[j] * bt * eg, v[j] * bt], axis=1).astype(BF16),
                    pj, beta_b, exp_g)
        wu = _each(lambda t, r: _dot(t.astype(BF16), r).astype(BF16), tmat, rhs)
        qkm = _each(lambda m, j, dc: jnp.where(m, qk[j] * dc, 0.0).astype(BF16), incl, pj, decay)
        ao = _each(_dot, qkm, wu)
        kd = _each(lambda j, gt, g: (k[j] * jnp.exp(gt - g)).astype(BF16), pj, gtot, gc)
        kbm = _each(_dot_tn, kd, wu)
        for p, (j, d) in enumerate(probs):
            kn_ref, bn_ref, a_ref, o0_ref, gl_ref = outs[d]
            r2 = pl.multiple_of(ci[j] * 2 * c, 2 * c)
            kn_ref[0, pl.ds(r2, 2 * c), :] = kbm[p][:, :HEAD_DIM].astype(BF16)
            bn_ref[0, pl.ds(r2, 2 * c), :] = kbm[p][:, HEAD_DIM:].astype(BF16)
            a_ref[0, pl.ds(r0[j], c), :] = (q[j] * exp_g[p] - ao[p][:, :HEAD_DIM]).astype(BF16)
            o0_ref[0, pl.ds(r0[j], c), :] = ao[p][:, HEAD_DIM:].astype(BF16)
            gl_ref[0, pl.ds(pl.multiple_of(ci[j] * 8, 8), 8), :] = jnp.broadcast_to(jnp.exp(gtot[p]), (8, HEAD_DIM))
        return carry

    lax.fori_loop(0, PREP_NC // nb, body, 0)


def _dn_prep(proj3, ab3, conv_h, a_log, dt_bias):
    b, seq, _ = proj3.shape
    seg = PREP_SEG
    ns = seq // seg
    hb16 = seg // 16
    last16 = seq // 16 - 1
    smem = pl.BlockSpec(memory_space=pltpu.SMEM)
    tok = lambda bi, h, s: (bi, s, h)
    big = jax.ShapeDtypeStruct((b, 2 * seq, DN_WIDTH), BF16)
    med = jax.ShapeDtypeStruct((b, seq, DN_WIDTH), BF16)
    gls = jax.ShapeDtypeStruct((b, seq // DN_CHUNK * 8, DN_WIDTH), F32)
    per_dir_specs = [pl.BlockSpec((1, 2 * seg, HEAD_DIM), tok), pl.BlockSpec((1, 2 * seg, HEAD_DIM), tok),
                     pl.BlockSpec((1, seg, HEAD_DIM), tok), pl.BlockSpec((1, seg, HEAD_DIM), tok),
                     pl.BlockSpec((1, PREP_NC * 8, HEAD_DIM), tok)]
    return pl.pallas_call(
        _dn_prep_kernel,
        grid=(b, DN_HEADS, ns),
        in_specs=[smem, smem,
                  pl.BlockSpec((1, seg, DN_HEAD_COLS), tok),
                  pl.BlockSpec((1, 16, DN_HEAD_COLS), lambda bi, h, s: (bi, jnp.maximum(s * hb16 - 1, 0), h)),
                  pl.BlockSpec((1, 16, DN_HEAD_COLS), lambda bi, h, s: (bi, jnp.minimum((s + 1) * hb16, last16), h)),
                  pl.BlockSpec((1, seg, AB_COLS), lambda bi, h, s: (bi, s, 0)),
                  pl.BlockSpec((1, 8, DN_HEAD_COLS), lambda bi, h, s: (h, 0, 0))],
        out_specs=per_dir_specs + per_dir_specs,
        out_shape=[big, big, med, med, gls] * 2,
        scratch_shapes=[pltpu.VMEM((seg + 16, DN_HEAD_COLS), F32)],
        compiler_params=_cparams(("parallel", "parallel", "parallel")),
    )(a_log, dt_bias, proj3, proj3, proj3, ab3, conv_h)


SCAN_HB = 4


def _dn_scan_kernel(knf_ref, bnf_ref, af_ref, o0f_ref, glf_ref, knb_ref, bnb_ref, ab_ref, o0b_ref, glb_ref,
                    of_ref, ob_ref, s_scr, *, nc):
    c = DN_CHUNK

    @pl.when(pl.program_id(2) == 0)
    def _():
        s_scr[...] = jnp.zeros(s_scr.shape, F32)

    def chain(idx, ci, kn_ref, bn_ref, a_ref, o0_ref, gl_ref, o_ref, hh):
        sl = slice(hh * HEAD_DIM, (hh + 1) * HEAD_DIM)
        r2 = pl.multiple_of(ci * 2 * c, 2 * c)
        r1 = pl.multiple_of(ci * c, c)
        lhs = jnp.concatenate([kn_ref[0, pl.ds(r2, 2 * c), sl], a_ref[0, pl.ds(r1, c), sl]], axis=0)
        st = s_scr[idx]
        t1 = _dot(lhs, st.astype(BF16))
        gl = gl_ref[0, pl.ds(pl.multiple_of(ci * 8, 8), 8), sl][0:1, :]
        s_scr[idx] = gl * st - t1[:2 * c] + bn_ref[0, pl.ds(r2, 2 * c), sl].astype(F32)
        o_ref[0, pl.ds(r1, c), sl] = (t1[2 * c:] + o0_ref[0, pl.ds(r1, c), sl].astype(F32)).astype(o_ref.dtype)

    def body(ci, carry):
        cb = nc - 1 - ci
        for hh in range(SCAN_HB):
            chain(hh, ci, knf_ref, bnf_ref, af_ref, o0f_ref, glf_ref, of_ref, hh)
            chain(SCAN_HB + hh, cb, knb_ref, bnb_ref, ab_ref, o0b_ref, glb_ref, ob_ref, hh)
        return carry

    lax.fori_loop(0, nc, body, 0)


def _dn_scan(prep):
    knf, bnf, af, o0f, glf, knb, bnb, ab_, o0b, glb = prep
    b, seq, _ = af.shape
    seg = min(1024, seq)
    nc = seg // DN_CHUNK
    ns = seq // seg
    w = SCAN_HB * HEAD_DIM
    fwd = lambda bi, hb, s: (bi, s, hb)
    bwd = lambda bi, hb, s: (bi, ns - 1 - s, hb)

    def specs(im):
        return [pl.BlockSpec((1, 2 * seg, w), im), pl.BlockSpec((1, 2 * seg, w), im),
                pl.BlockSpec((1, seg, w), im), pl.BlockSpec((1, seg, w), im),
                pl.BlockSpec((1, nc * 8, w), im)]

    out = jax.ShapeDtypeStruct((b, seq, DN_WIDTH), BF16)
    return pl.pallas_call(
        functools.partial(_dn_scan_kernel, nc=nc),
        grid=(b, DN_HEADS // SCAN_HB, ns),
        in_specs=specs(fwd) + specs(bwd),
        out_specs=[pl.BlockSpec((1, seg, w), fwd), pl.BlockSpec((1, seg, w), bwd)],
        out_shape=[out, out],
        scratch_shapes=[pltpu.VMEM((2 * SCAN_HB, HEAD_DIM, HEAD_DIM), F32)],
        compiler_params=_cparams(("parallel", "parallel", "arbitrary")),
    )(knf, bnf, af, o0f, glf, knb, bnb, ab_, o0b, glb)


def _attn_kernel(q_ref, kp_ref, km_ref, kn_ref, vp_ref, vm_ref, vn_ref, o_ref, lse_ref, *, tq):
    i = pl.program_id(2)
    nq = pl.num_programs(2)
    scale = HEAD_DIM ** -0.5
    row_m = lax.broadcasted_iota(jnp.int32, (tq, tq), 0)
    col_m = lax.broadcasted_iota(jnp.int32, (tq, tq), 1)
    mask_m = jnp.abs(row_m - col_m) <= HALO
    row_h = lax.broadcasted_iota(jnp.int32, (tq, HALO), 0)
    col_h = lax.broadcasted_iota(jnp.int32, (tq, HALO), 1)
    mask_p = jnp.logical_and(row_h <= col_h, i > 0)
    mask_n = jnp.logical_and(row_h - col_h >= tq - HALO, i < nq - 1)
    lane = lax.broadcasted_iota(jnp.int32, (tq, HEAD_DIM), 1)
    lse_tile = jnp.zeros((tq, HEAD_DIM), F32)
    for hh in range(HPG):
        sl = slice(hh * HEAD_DIM, (hh + 1) * HEAD_DIM)
        q = q_ref[0, 0, :, sl]
        sp = jnp.where(mask_p, _dot_nt(q, kp_ref[0, 0, :, sl]) * scale, MASK_VALUE)
        sm = jnp.where(mask_m, _dot_nt(q, km_ref[0, 0, :, sl]) * scale, MASK_VALUE)
        sn = jnp.where(mask_n, _dot_nt(q, kn_ref[0, 0, :, sl]) * scale, MASK_VALUE)
        m = jnp.maximum(jnp.max(sm, axis=-1, keepdims=True),
                        jnp.maximum(jnp.max(sp, axis=-1, keepdims=True), jnp.max(sn, axis=-1, keepdims=True)))
        pp = jnp.exp(sp - m)
        pm = jnp.exp(sm - m)
        pn = jnp.exp(sn - m)
        den = (jnp.sum(pm, axis=-1, keepdims=True) + jnp.sum(pp, axis=-1, keepdims=True)
               + jnp.sum(pn, axis=-1, keepdims=True))
        o = (_dot(pm.astype(BF16), vm_ref[0, 0, :, sl]) + _dot(pp.astype(BF16), vp_ref[0, 0, :, sl])
             + _dot(pn.astype(BF16), vn_ref[0, 0, :, sl]))
        o_ref[0, 0, :, sl] = (o / den).astype(o_ref.dtype)
        lse_tile = jnp.where(lane == hh, m + jnp.log(den), lse_tile)
    lse_ref[0, 0] = lse_tile


def _attention_group(at4, cq):
    b, dil, m, _ = at4.shape
    tq = min(256, m)
    nq = m // tq
    hq = tq // HALO
    lastb = m // HALO - 1

    def main(off):
        return pl.BlockSpec((1, 1, tq, ATTN_OUT), lambda bi, r, i: (bi, r, i, cq + off))

    def prev(off):
        return pl.BlockSpec((1, 1, HALO, ATTN_OUT), lambda bi, r, i: (bi, r, jnp.maximum(i * hq - 1, 0), cq + off))

    def nxt(off):
        return pl.BlockSpec((1, 1, HALO, ATTN_OUT),
                            lambda bi, r, i: (bi, r, jnp.minimum((i + 1) * hq, lastb), cq + off))

    return pl.pallas_call(
        functools.partial(_attn_kernel, tq=tq),
        grid=(b, dil, nq),
        in_specs=[main(0), prev(1), main(1), nxt(1), prev(2), main(2), nxt(2)],
        out_specs=[pl.BlockSpec((1, 1, tq, ATTN_OUT), lambda bi, r, i: (bi, r, i, 0)),
                   pl.BlockSpec((1, 1, tq, HEAD_DIM), lambda bi, r, i: (bi, r, i, 0))],
        out_shape=[jax.ShapeDtypeStruct((b, dil, m, ATTN_OUT), BF16),
                   jax.ShapeDtypeStruct((b, dil, m, HEAD_DIM), F32)],
        compiler_params=_cparams(("parallel", "parallel", "parallel")),
    )(at4, at4, at4, at4, at4, at4, at4)


MIX_TM = 256


def _mix_kernel(x_ref, mod_ref, of_ref, ob_ref, z_ref, gdn_ref, gat_ref,
                o1_ref, o2_ref, o3_ref, l1_ref, l2_ref, l3_ref,
                dnw_ref, wdn_ref, wat_ref, wout_ref, npost_ref, out_ref, a_scr, b_scr,
                o2_scr, o3_scr, l2_scr, l3_scr):
    tm = a_scr.shape[0]
    for src, lsrc, dst, ldst, gi in ((o2_ref, l2_ref, o2_scr, l2_scr, 1), (o3_ref, l3_ref, o3_scr, l3_scr, 2)):
        dil = ATTN_GROUPS[gi][1]
        for r in range(dil):
            for hh in range(HPG):
                dst[hh, pl.ds(r, tm // dil, stride=dil), :] = (
                    src[0, r, :, hh * HEAD_DIM:(hh + 1) * HEAD_DIM].astype(F32))
            ldst[pl.ds(r, tm // dil, stride=dil), :] = lsrc[0, r]
    dnw = dnw_ref[...]
    for h in range(DN_HEADS):
        sl = slice(h * HEAD_DIM, (h + 1) * HEAD_DIM)
        o = of_ref[:, sl].astype(F32) + ob_ref[:, sl].astype(F32)
        y = o * lax.rsqrt(jnp.mean(o * o, axis=-1, keepdims=True) + NORM_EPS) * dnw
        a_scr[:, sl] = (y * _silu(z_ref[:, sl].astype(F32))).astype(BF16)
    l1 = l1_ref[...]
    l2 = l2_scr[...]
    l3 = l3_scr[...]
    lm = jnp.maximum(l1, jnp.maximum(l2, l3))
    e1 = jnp.exp(l1 - lm)
    e2 = jnp.exp(l2 - lm)
    e3 = jnp.exp(l3 - lm)
    inv = 1.0 / (e1 + e2 + e3)
    w1 = e1 * inv
    w2 = e2 * inv
    w3 = e3 * inv
    for h in range(HPG):
        sl = slice(h * HEAD_DIM, (h + 1) * HEAD_DIM)
        b_scr[:, sl] = (w1[:, h:h + 1] * o1_ref[:, sl].astype(F32) + w2[:, h:h + 1] * o2_scr[h]
                        + w3[:, h:h + 1] * o3_scr[h]).astype(BF16)
    y_dn = _dot(a_scr[...], wdn_ref[...])
    y_at = _dot(b_scr[...], wat_ref[...])
    mixed_in = (jax.nn.sigmoid(gdn_ref[...].astype(F32)) * y_dn
                + jax.nn.sigmoid(gat_ref[...].astype(F32)) * y_at).astype(BF16)
    mixed = _dot(mixed_in, wout_ref[...])
    nrm = mixed * lax.rsqrt(jnp.mean(mixed * mixed, axis=-1, keepdims=True) + NORM_EPS) * npost_ref[...]
    out_ref[...] = x_ref[...] + mod_ref[0, 2:3, :] * nrm


def _mix(x2, mod3, proj2, o_f, o_b, at_o, at_lse, dn_norm_w, w_dn_out, w_at_out, w_out, norm_post, seq):
    rows = x2.shape[0]
    tm = MIX_TM
    nt = seq // tm
    rowblk = lambda w: pl.BlockSpec((tm, w), lambda i: (i, 0))
    const = lambda shape: pl.BlockSpec(shape, lambda i: (0, 0))
    d1, d2 = ATTN_GROUPS[1][1], ATTN_GROUPS[2][1]
    dilblk = lambda dil, w: pl.BlockSpec((1, dil, tm // dil, w), lambda i: (i // nt, 0, i % nt, 0))
    return pl.pallas_call(
        _mix_kernel,
        grid=(rows // tm,),
        in_specs=[rowblk(D_MODEL),
                  pl.BlockSpec((1, N_MOD, D_MODEL), lambda i: (i // nt, 0, 0)),
                  rowblk(DN_WIDTH), rowblk(DN_WIDTH),
                  pl.BlockSpec((tm, DN_WIDTH), lambda i: (i, COL_Z // DN_WIDTH)),
                  pl.BlockSpec((tm, D_MODEL), lambda i: (i, COL_MERGE // D_MODEL)),
                  pl.BlockSpec((tm, D_MODEL), lambda i: (i, COL_MERGE // D_MODEL + 1)),
                  rowblk(ATTN_OUT), dilblk(d1, ATTN_OUT), dilblk(d2, ATTN_OUT),
                  rowblk(HEAD_DIM), dilblk(d1, HEAD_DIM), dilblk(d2, HEAD_DIM),
                  const((1, HEAD_DIM)), const((DN_WIDTH, D_MODEL)), const((ATTN_OUT, D_MODEL)),
                  const((D_MODEL, D_MODEL)), const((1, D_MODEL))],
        out_specs=rowblk(D_MODEL),
        out_shape=jax.ShapeDtypeStruct((rows, D_MODEL), F32),
        scratch_shapes=[pltpu.VMEM((tm, DN_WIDTH), BF16), pltpu.VMEM((tm, ATTN_OUT), BF16),
                        pltpu.VMEM((HPG, tm, HEAD_DIM), F32), pltpu.VMEM((HPG, tm, HEAD_DIM), F32),
                        pltpu.VMEM((tm, HEAD_DIM), F32), pltpu.VMEM((tm, HEAD_DIM), F32)],
        compiler_params=_cparams(("parallel",)),
    )(x2, mod3, o_f, o_b, proj2, proj2, proj2, at_o[0].reshape(rows, ATTN_OUT), at_o[1], at_o[2],
      at_lse[0].reshape(rows, HEAD_DIM), at_lse[1], at_lse[2],
      dn_norm_w.reshape(1, HEAD_DIM), w_dn_out, w_at_out, w_out, norm_post.reshape(1, D_MODEL))


FFN_TM = 512
FFN_TF = 512


def _ffn_kernel(x_ref, mod_ref, npre_ref, w1_ref, w2_ref, npost_ref, out_ref, h_scr, acc_scr):
    j = pl.program_id(1)

    @pl.when(j == 0)
    def _():
        x = x_ref[...]
        y = x * lax.rsqrt(jnp.mean(x * x, axis=-1, keepdims=True) + NORM_EPS) * npre_ref[...]
        h_scr[...] = (y * (1.0 + mod_ref[0, 4:5, :]) + mod_ref[0, 3:4, :]).astype(BF16)
        acc_scr[...] = jnp.zeros(acc_scr.shape, F32)

    t = jnp.maximum(_dot(h_scr[...], w1_ref[...]), 0.0)
    acc_scr[...] += _dot((t * t).astype(BF16), w2_ref[...])

    @pl.when(j == pl.num_programs(1) - 1)
    def _():
        f = acc_scr[...]
        nrm = f * lax.rsqrt(jnp.mean(f * f, axis=-1, keepdims=True) + NORM_EPS) * npost_ref[...]
        out_ref[...] = x_ref[...] + mod_ref[0, 5:6, :] * nrm


def _ffn(x2, mod3, norm_pre, w_ff1, w_ff2, norm_post, seq):
    rows = x2.shape[0]
    tm = FFN_TM
    nt = seq // tm
    return pl.pallas_call(
        _ffn_kernel,
        grid=(rows // tm, D_FF // FFN_TF),
        in_specs=[pl.BlockSpec((tm, D_MODEL), lambda i, j: (i, 0)),
                  pl.BlockSpec((1, N_MOD, D_MODEL), lambda i, j: (i // nt, 0, 0)),
                  pl.BlockSpec((1, D_MODEL), lambda i, j: (0, 0)),
                  pl.BlockSpec((D_MODEL, FFN_TF), lambda i, j: (0, j)),
                  pl.BlockSpec((FFN_TF, D_MODEL), lambda i, j: (j, 0)),
                  pl.BlockSpec((1, D_MODEL), lambda i, j: (0, 0))],
        out_specs=pl.BlockSpec((tm, D_MODEL), lambda i, j: (i, 0)),
        out_shape=jax.ShapeDtypeStruct((rows, D_MODEL), F32),
        scratch_shapes=[pltpu.VMEM((tm, D_MODEL), BF16), pltpu.VMEM((tm, D_MODEL), F32)],
        compiler_params=_cparams(("parallel", "arbitrary")),
    )(x2, mod3, norm_pre.reshape(1, D_MODEL), w_ff1, w_ff2, norm_post.reshape(1, D_MODEL))


def _prep_in_weights(w_in, conv_w):
    o = 0
    dn_qkv = w_in[:, o:o + 3 * DN_WIDTH]; o += 3 * DN_WIDTH
    dn_z = w_in[:, o:o + DN_WIDTH]; o += DN_WIDTH
    dn_ab = w_in[:, o:o + 4 * DN_HEADS]; o += 4 * DN_HEADS
    at_qkv = w_in[:, o:o + 3 * ATTN_WIDTH]; o += 3 * ATTN_WIDTH
    merge = w_in[:, o:o + 2 * D_MODEL]
    dn_ph = dn_qkv.reshape(D_MODEL, 3, DN_HEADS, HEAD_DIM).transpose(0, 2, 1, 3).reshape(D_MODEL, 3 * DN_WIDTH)
    at_pg = (at_qkv.reshape(D_MODEL, 3, N_GROUPS, ATTN_OUT).transpose(0, 2, 1, 3)
             .reshape(D_MODEL, 3 * ATTN_WIDTH))
    w_main = jnp.concatenate([dn_ph, dn_z, merge, at_pg], axis=1).astype(BF16)
    w_ab = jnp.pad(dn_ab, ((0, 0), (0, AB_COLS - 4 * DN_HEADS))).astype(BF16)
    conv_h = conv_w.reshape(DN_CONV, 3, DN_HEADS, HEAD_DIM).transpose(2, 0, 1, 3).reshape(DN_HEADS, DN_CONV, DN_HEAD_COLS)
    conv_h = jnp.pad(conv_h, ((0, 0), (0, 8 - DN_CONV), (0, 0)))
    return w_main, w_ab, conv_h


def _rope_tables(seq):
    half = HEAD_DIM // 2
    inv_freq = ROPE_THETA ** (-jnp.arange(half, dtype=F32) / half)
    ang = jnp.arange(seq, dtype=F32)[:, None] * inv_freq[None, :]
    cos = jnp.cos(ang)
    sin = jnp.sin(ang)
    return jnp.concatenate([cos, cos], axis=1), jnp.concatenate([-sin, sin], axis=1)


def _group_forward(x, mod, wts):
    b, seq, _ = x.shape
    rows = b * seq
    x2 = x.reshape(rows, D_MODEL)
    mod3 = mod.reshape(b, N_MOD, D_MODEL)
    cos_t, sin_t = _rope_tables(seq)
    proj2, at1, at2, ab2 = _in_proj(x2, mod3, wts["norm_pre_mix"], wts["w_main"], wts["w_ab"], cos_t, sin_t, seq)
    proj3 = proj2.reshape(b, seq, W_PROJ)
    prep = _dn_prep(proj3, ab2.reshape(b, seq, AB_COLS), wts["conv_h"], wts["A_log"], wts["dt_bias"])
    o_f, o_b = _dn_scan(prep)
    at = [_attention_group(proj2.reshape(b, 1, seq, W_PROJ), COL_AT // ATTN_OUT),
          _attention_group(at1, 0), _attention_group(at2, 0)]
    x1 = _mix(x2, mod3, proj2, o_f.reshape(rows, DN_WIDTH), o_b.reshape(rows, DN_WIDTH),
              [a[0] for a in at], [a[1] for a in at], wts["dn_norm_w"], wts["w_dn_out"], wts["w_at_out"],
              wts["w_out"], wts["norm_post_mix"], seq)
    y = _ffn(x1, mod3, wts["norm_pre_ffn"], wts["w_ff1"], wts["w_ff2"], wts["norm_post_ffn"], seq)
    return y.reshape(b, seq, D_MODEL)


def kernel(x_prompt, x_sample, c_prompt, c_sample, w_ada, b_ada, norm_pre_mix, norm_post_mix, norm_pre_ffn,
           norm_post_ffn, w_in, conv_w, A_log, dt_bias, dn_norm_w, w_dn_out, w_at_out, w_out, w_ff1, w_ff2):
    xs = (x_prompt, x_sample)
    nbp = c_prompt.shape[0]
    c_all = jnp.concatenate([c_prompt, c_sample], axis=0)
    for l in range(w_ada.shape[0]):
        w_main, w_ab, conv_h = _prep_in_weights(w_in[l], conv_w[l])
        wts = dict(norm_pre_mix=norm_pre_mix[l], norm_post_mix=norm_post_mix[l], norm_pre_ffn=norm_pre_ffn[l],
                   norm_post_ffn=norm_post_ffn[l], w_main=w_main, w_ab=w_ab, conv_h=conv_h, A_log=A_log[l],
                   dt_bias=dt_bias[l], dn_norm_w=dn_norm_w[l], w_dn_out=w_dn_out[l].astype(BF16),
                   w_at_out=w_at_out[l].astype(BF16), w_out=w_out[l].astype(BF16),
                   w_ff1=w_ff1[l].astype(BF16), w_ff2=w_ff2[l].astype(BF16))
        mod_all = _ada_mod(c_all, w_ada[l], b_ada[l])
        xs = (_group_forward(xs[0], mod_all[:nbp], wts), _group_forward(xs[1], mod_all[nbp:], wts))
    return xs
```

```python
import functools

import jax
import jax.numpy as jnp
from jax import lax
from jax.experimental import pallas as pl
from jax.experimental.pallas import tpu as pltpu

F32 = jnp.float32
BF16 = jnp.bfloat16

D_MODEL = 2048
HEAD_DIM = 128
DN_HEADS = 8
DN_WIDTH = DN_HEADS * HEAD_DIM
DN_CONV = 5
DN_CHUNK = 64
ATTN_GROUPS = ((128, 1), (512, 4), (2048, 16))
N_GROUPS = 3
HPG = 4
ATTN_HEADS = N_GROUPS * HPG
ATTN_WIDTH = ATTN_HEADS * HEAD_DIM
ATTN_OUT = HPG * HEAD_DIM
ROPE_THETA = 10000.0
D_FF = 4 * D_MODEL
N_MOD = 6
NORM_EPS = 1e-6
MASK_VALUE = -1e30

DN_HEAD_COLS = 3 * HEAD_DIM
AT_GROUP_COLS = 3 * ATTN_OUT
COL_AT0_QK = 0
COL_Z = COL_AT0_QK + 2 * ATTN_OUT
COL_MERGE = COL_Z + DN_WIDTH
COL_DN = COL_MERGE + 2 * D_MODEL
COL_AT0_V = COL_DN + 3 * DN_WIDTH
W_PROJ = COL_AT0_V + ATTN_OUT
AB_COLS = 128

PROJ_TN = 512
DIL_TILES = 2 * AT_GROUP_COLS // PROJ_TN
MAIN_TILES = W_PROJ // PROJ_TN
ROT_TILES = DIL_TILES + COL_Z // PROJ_TN
W_MAIN = (DIL_TILES + MAIN_TILES) * PROJ_TN
HALO = 64

VMEM_LIMIT = 56 * 1024 * 1024


def _cparams(sem):
    return pltpu.CompilerParams(dimension_semantics=sem, vmem_limit_bytes=VMEM_LIMIT)


def _dot(a, b):
    return jnp.dot(a, b, preferred_element_type=F32)


def _dot_nt(a, b):
    return lax.dot_general(a, b, (((1,), (1,)), ((), ())), preferred_element_type=F32)


def _dot_tn(a, b):
    return lax.dot_general(a, b, (((0,), (0,)), ((), ())), preferred_element_type=F32)


def _silu(x):
    return x * jax.nn.sigmoid(x)


def _mod_kernel(c_ref, w_ref, b_ref, o_ref):
    s = _silu(c_ref[...]).astype(BF16)
    o_ref[...] = _dot(s, w_ref[...].astype(BF16)) + b_ref[...]


def _ada_mod(c_all, w_ada, b_ada):
    nb, n = c_all.shape[0], w_ada.shape[1]
    tn = 512
    return pl.pallas_call(
        _mod_kernel,
        grid=(n // tn,),
        in_specs=[pl.BlockSpec((nb, D_MODEL), lambda j: (0, 0)),
                  pl.BlockSpec((D_MODEL, tn), lambda j: (0, j)),
                  pl.BlockSpec((1, tn), lambda j: (0, j))],
        out_specs=pl.BlockSpec((nb, tn), lambda j: (0, j)),
        out_shape=jax.ShapeDtypeStruct((nb, n), F32),
        compiler_params=_cparams(("arbitrary",)),
    )(c_all, w_ada, b_ada.reshape(1, n))


def _inproj_kernel(x_ref, mod_ref, nw_ref, w_ref, wab_ref, cos_ref, sin_ref, o_ref, o1_ref, o2_ref, ab_ref,
                   h_scr, acc_scr):
    j = pl.program_id(1)
    tm = acc_scr.shape[1]
    nslab = PROJ_TN // HEAD_DIM

    @pl.when(j == 0)
    def _():
        x = x_ref[...]
        ms = jnp.mean(x * x, axis=-1, keepdims=True)
        y = x * lax.rsqrt(ms + NORM_EPS) * nw_ref[...]
        h = y * (1.0 + mod_ref[0, 1:2, :]) + mod_ref[0, 0:1, :]
        hb = h.astype(BF16)
        h_scr[...] = hb
        ab_ref[...] = _dot(hb, wab_ref[...])

    acc = _dot(h_scr[...], w_ref[...])
    is_rot = jnp.logical_and(j < ROT_TILES, lax.rem(j, 3) != 2)
    is_dil = j < DIL_TILES

    def rotated(a):
        return a * cos_ref[...] + pltpu.roll(a, HEAD_DIM // 2, axis=1) * sin_ref[...]

    o_ref[...] = acc.astype(BF16)

    @pl.when(jnp.logical_and(jnp.logical_not(is_dil), is_rot))
    def _():
        for hh in range(nslab):
            sl = slice(hh * HEAD_DIM, (hh + 1) * HEAD_DIM)
            o_ref[:, sl] = rotated(acc[:, sl]).astype(BF16)

    @pl.when(jnp.logical_and(is_dil, is_rot))
    def _():
        for hh in range(nslab):
            acc_scr[hh] = rotated(acc[:, hh * HEAD_DIM:(hh + 1) * HEAD_DIM])

    @pl.when(jnp.logical_and(is_dil, jnp.logical_not(is_rot)))
    def _():
        for hh in range(nslab):
            acc_scr[hh] = acc[:, hh * HEAD_DIM:(hh + 1) * HEAD_DIM]

    for o_dil, gi in ((o1_ref, 1), (o2_ref, 2)):
        dil = ATTN_GROUPS[gi][1]
        t0 = 3 * (gi - 1)

        @pl.when(jnp.logical_and(j >= t0, j < t0 + 3))
        def _(o_dil=o_dil, dil=dil):
            for r in range(dil):
                for hh in range(nslab):
                    o_dil[0, r, :, hh * HEAD_DIM:(hh + 1) * HEAD_DIM] = (
                        acc_scr[hh, pl.ds(r, tm // dil, stride=dil), :].astype(BF16))


def _in_proj(x2, mod3, norm_w, w_main, w_ab, cos_t, sin_t, seq):
    rows = x2.shape[0]
    b = rows // seq
    tm = min(1024, seq)
    nt = seq // tm
    d1, d2 = ATTN_GROUPS[1][1], ATTN_GROUPS[2][1]

    def dil_spec(dil, t0):
        return pl.BlockSpec((1, dil, tm // dil, PROJ_TN),
                            lambda i, j: (i // nt, 0, i % nt, jnp.clip(j - t0, 0, 2)))

    return pl.pallas_call(
        _inproj_kernel,
        grid=(rows // tm, W_MAIN // PROJ_TN),
        in_specs=[pl.BlockSpec((tm, D_MODEL), lambda i, j: (i, 0)),
                  pl.BlockSpec((1, N_MOD, D_MODEL), lambda i, j: (i // nt, 0, 0)),
                  pl.BlockSpec((1, D_MODEL), lambda i, j: (0, 0)),
                  pl.BlockSpec((D_MODEL, PROJ_TN), lambda i, j: (0, j)),
                  pl.BlockSpec((D_MODEL, AB_COLS), lambda i, j: (0, 0)),
                  pl.BlockSpec((tm, HEAD_DIM), lambda i, j: (i % nt, 0)),
                  pl.BlockSpec((tm, HEAD_DIM), lambda i, j: (i % nt, 0))],
        out_specs=[pl.BlockSpec((tm, PROJ_TN), lambda i, j: (i, jnp.maximum(j - DIL_TILES, 0))),
                   dil_spec(d1, 0), dil_spec(d2, 3),
                   pl.BlockSpec((tm, AB_COLS), lambda i, j: (i, 0))],
        out_shape=[jax.ShapeDtypeStruct((rows, W_PROJ), BF16),
                   jax.ShapeDtypeStruct((b, d1, seq // d1, AT_GROUP_COLS), BF16),
                   jax.ShapeDtypeStruct((b, d2, seq // d2, AT_GROUP_COLS), BF16),
                   jax.ShapeDtypeStruct((rows, AB_COLS), F32)],
        scratch_shapes=[pltpu.VMEM((tm, D_MODEL), BF16), pltpu.VMEM((PROJ_TN // HEAD_DIM, tm, HEAD_DIM), F32)],
        compiler_params=_cparams(("parallel", "arbitrary")),
    )(x2, mod3, norm_w.reshape(1, D_MODEL), w_main, w_ab, cos_t, sin_t)


PREP_SEG = 512
PREP_NC = PREP_SEG // DN_CHUNK
PREP_UNROLL = 8


def _each(f, *lists):
    return [f(*args) for args in zip(*lists)]


def _aligned(x, m):
    return x if isinstance(x, int) else pl.multiple_of(x, m)


def _tri_inverse_pairs(lms, eye2, pair_mask, level_masks, bd_mask):
    l_bd = _each(lambda lm: jnp.concatenate([lm, lm], axis=0), lms)
    xs = _each(lambda lm: eye2 - jnp.where(pair_mask, lm, 0.0), lms)
    for mask in level_masks:
        n_bd = _each(lambda lb: jnp.where(mask, lb, 0.0).astype(BF16), l_bd)
        ys = _each(lambda x, n: _dot(x.astype(BF16), n), xs, n_bd)
        x_bd = _each(lambda x: jnp.where(bd_mask, jnp.concatenate([x, x], axis=0), 0.0).astype(BF16), xs)
        xs = _each(lambda x, y, xb: x - _dot(y.astype(BF16), xb), xs, ys, x_bd)
    return xs


def _dn_prep_kernel(gp_ref, main_ref, prev_ref, next_ref, ab_ref, cw_ref,
                    knf_ref, bnf_ref, af_ref, o0f_ref, glf_ref,
                    knb_ref, bnb_ref, ab_out_ref, o0b_ref, glb_ref, xe_scr, gb_scr):
    h = pl.program_id(1)
    s = pl.program_id(2)
    ns = pl.num_programs(2)
    c = DN_CHUNK
    seg = PREP_SEG

    pv = prev_ref[0].astype(F32)[8:16, :] * (s > 0).astype(F32)
    nx = next_ref[0].astype(F32)[0:8, :] * (s < ns - 1).astype(F32)
    xe_scr[0:8, :] = pv
    xe_scr[8:seg + 8, :] = main_ref[0].astype(F32)
    xe_scr[seg + 8:seg + 16, :] = nx

    row = lax.broadcasted_iota(jnp.int32, (c, 2 * c), 0)
    lane = lax.broadcasted_iota(jnp.int32, (c, 2 * c), 1)
    col = jnp.bitwise_and(lane, c - 1)
    is_f = lane < c
    is_b = jnp.logical_not(is_f)
    incl2 = jnp.logical_or(jnp.logical_and(is_f, row >= col), jnp.logical_and(is_b, row <= col))
    strict2 = jnp.logical_or(jnp.logical_and(is_f, row > col), jnp.logical_and(is_b, row < col))
    eye2 = (row == col).astype(F32)
    tri_cat = ((col <= row).astype(BF16), (col >= row).astype(BF16))
    gr_mask = (jnp.logical_and(is_f, row <= col), jnp.logical_and(is_b, row >= col))
    ones4 = jnp.ones((c, 4 * c), BF16)
    row128 = lax.broadcasted_iota(jnp.int32, (2 * c, 2 * c), 0)
    lane128 = lax.broadcasted_iota(jnp.int32, (2 * c, 2 * c), 1)
    bd_mask = jnp.bitwise_and(row128, c) == jnp.bitwise_and(lane128, c)
    r64 = jnp.bitwise_and(row128, c - 1)
    c64 = jnp.bitwise_and(lane128, c - 1)

    def joins(ri, ci_, lg):
        return jnp.logical_and(jnp.right_shift(ri, lg) != jnp.right_shift(ci_, lg),
                               jnp.right_shift(ri, lg + 1) == jnp.right_shift(ci_, lg + 1))

    pair_mask = joins(row, col, 0)
    level_masks = [jnp.logical_and(bd_mask, joins(r64, c64, lg)) for lg in range(1, 6)]
    ab_all = ab_ref[0]
    z = ab_all + gp_ref[1:2, :]
    softplus = jnp.maximum(z, 0.0) + jnp.log(1.0 + jnp.exp(-jnp.abs(z)))
    lane_seg = lax.broadcasted_iota(jnp.int32, (seg, AB_COLS), 1)
    gb_scr[...] = jnp.where(lane_seg < 2 * DN_HEADS, -jnp.exp(gp_ref[0:1, :]) * softplus, jax.nn.sigmoid(ab_all))
    cw = cw_ref[0]
    outs = ((knf_ref, bnf_ref, af_ref, o0f_ref, glf_ref), (knb_ref, bnb_ref, ab_out_ref, o0b_ref, glb_ref))

    nb = PREP_UNROLL
    probs = [(j, d) for j in range(nb) for d in range(2)]
    pj = [j for j, _ in probs]
    pd = [d for _, d in probs]

    def body(it, carry):
        ci = [it * nb + j for j in range(nb)]
        r0 = [_aligned(cj * c, c) for cj in ci]
        xs = []
        for j in range(nb):
            if isinstance(r0[j], int):
                taps = [xe_scr[r0[j] + 6 + t:r0[j] + 6 + t + c, :] for t in range(DN_CONV)]
            else:
                win = xe_scr[pl.ds(r0[j], c + 16), :]
                taps = [win[6 + t:6 + t + c, :] for t in range(DN_CONV)]
            acc = cw[0:1, :] * taps[0]
            for t in range(1, DN_CONV):
                acc = acc + cw[t:t + 1, :] * taps[t]
            xs.append(_silu(acc))
        q = [x[:, :HEAD_DIM] for x in xs]
        k = [x[:, HEAD_DIM:2 * HEAD_DIM] for x in xs]
        v = [x[:, 2 * HEAD_DIM:] for x in xs]
        q = _each(lambda t: t * lax.rsqrt(jnp.sum(t * t, axis=-1, keepdims=True) + NORM_EPS) * (HEAD_DIM ** -0.5), q)
        k = _each(lambda t: t * lax.rsqrt(jnp.sum(t * t, axis=-1, keepdims=True) + NORM_EPS), k)
        qb = _each(lambda t: t.astype(BF16), q)
        kbf = _each(lambda t: t.astype(BF16), k)
        kq = _each(lambda kb_, qb_: _dot_nt(jnp.concatenate([kb_, qb_], axis=0), jnp.concatenate([kb_, kb_], axis=0)),
                   kbf, qb)
        gbc = [gb_scr[pl.ds(r0[j], c), :] for j in range(nb)]

        def gate_cols(j, d):
            g_col = jnp.sum(jnp.where(lane == d * DN_HEADS + h, gbc[j], 0.0), axis=-1, keepdims=True)
            b_col = jnp.sum(jnp.where(lane == 2 * DN_HEADS + d * DN_HEADS + h, gbc[j], 0.0), axis=-1, keepdims=True)
            return jnp.broadcast_to(g_col, (c, HEAD_DIM)), jnp.broadcast_to(b_col, (c, HEAD_DIM))

        gates = _each(gate_cols, pj, pd)
        g_b = [g for g, _ in gates]
        beta_b = [bt for _, bt in gates]
        g_hi = _each(lambda g: g.astype(BF16).astype(F32), g_b)
        g_lo = _each(lambda g, gh: g - gh, g_b, g_hi)
        gc = _each(lambda d, gh, gl_: _dot(tri_cat[d], jnp.concatenate([gh, gl_], axis=0).astype(BF16)),
                   pd, g_hi, g_lo)
        gtot = _each(lambda g: jnp.sum(g, axis=0, keepdims=True), g_b)
        exp_g = _each(jnp.exp, gc)

        def row_form(j):
            pf, pb = 2 * j, 2 * j + 1
            parts = [jnp.where(gr_mask[0], g_hi[pf], 0.0), jnp.where(gr_mask[0], g_lo[pf], 0.0),
                     jnp.where(gr_mask[1], g_hi[pb], 0.0), jnp.where(gr_mask[1], g_lo[pb], 0.0)]
            return _dot(ones4, jnp.concatenate(parts, axis=0).astype(BF16))

        gr2 = [row_form(j) for j in range(nb)]
        gc2 = [jnp.where(is_f, gc[2 * j], gc[2 * j + 1]) for j in range(nb)]
        beta2 = [jnp.where(is_f, beta_b[2 * j], beta_b[2 * j + 1]) for j in range(nb)]
        decay2 = _each(lambda a, b_: jnp.where(incl2, jnp.exp(jnp.where(incl2, a - b_, 0.0)), 0.0), gc2, gr2)
        lm2 = _each(lambda bt, kq_, dc: jnp.where(strict2, bt * kq_[:c] * dc, 0.0), beta2, kq, decay2)
        tmat2 = _tri_inverse_pairs(lm2, eye2, pair_mask, level_masks, bd_mask)
        qkm2 = _each(lambda kq_, dc: jnp.where(incl2, kq_[c:] * dc, 0.0), kq, decay2)
        half = lambda x2, d: x2[:, d * c:(d + 1) * c].astype(BF16)
        rhs = _each(lambda j, bt, eg: jnp.concatenate([k[j] * bt * eg, v[j] * bt], axis=1).astype(BF16),
                    pj, beta_b, exp_g)
        wu = _each(lambda j, d, r: _dot(half(tmat2[j], d), r).astype(BF16), pj, pd, rhs)
        ao = _each(lambda j, d, w_: _dot(half(qkm2[j], d), w_), pj, pd, wu)
        kd = _each(lambda j, gt, g: (k[j] * jnp.exp(gt - g)).astype(BF16), pj, gtot, gc)
        kbm = _each(_dot_tn, kd, wu)
        for p, (j, d) in enumerate(probs):
            kn_ref, bn_ref, a_ref, o0_ref, gl_ref = outs[d]
            r2 = _aligned(ci[j] * 2 * c, 2 * c)
            kn_ref[0, pl.ds(r2, 2 * c), :] = kbm[p][:, :HEAD_DIM].astype(BF16)
            bn_ref[0, pl.ds(r2, 2 * c), :] = kbm[p][:, HEAD_DIM:].astype(BF16)
            a_ref[0, pl.ds(r0[j], c), :] = (q[j] * exp_g[p] - ao[p][:, :HEAD_DIM]).astype(BF16)
            o0_ref[0, pl.ds(r0[j], c), :] = ao[p][:, HEAD_DIM:].astype(BF16)
            gl_ref[0, pl.ds(_aligned(ci[j] * 8, 8), 8), :] = jnp.broadcast_to(jnp.exp(gtot[p]), (8, HEAD_DIM))
        return carry

    if PREP_NC == nb:
        body(0, 0)
    else:
        lax.fori_loop(0, PREP_NC // nb, body, 0)


def _dn_prep(proj3, ab3, conv_h, a_log, dt_bias):
    b, seq, _ = proj3.shape
    seg = PREP_SEG
    ns = seq // seg
    hb16 = seg // 16
    last16 = seq // 16 - 1
    gate_params = jnp.zeros((8, AB_COLS), F32)
    gate_params = gate_params.at[0, :2 * DN_HEADS].set(a_log.reshape(-1)).at[1, :2 * DN_HEADS].set(dt_bias.reshape(-1))
    tok = lambda bi, h, s: (bi, s, h)
    c0 = COL_DN // DN_HEAD_COLS
    big = jax.ShapeDtypeStruct((b, 2 * seq, DN_WIDTH), BF16)
    med = jax.ShapeDtypeStruct((b, seq, DN_WIDTH), BF16)
    gls = jax.ShapeDtypeStruct((b, seq // DN_CHUNK * 8, DN_WIDTH), F32)
    per_dir_specs = [pl.BlockSpec((1, 2 * seg, HEAD_DIM), tok), pl.BlockSpec((1, 2 * seg, HEAD_DIM), tok),
                     pl.BlockSpec((1, seg, HEAD_DIM), tok), pl.BlockSpec((1, seg, HEAD_DIM), tok),
                     pl.BlockSpec((1, PREP_NC * 8, HEAD_DIM), tok)]
    return pl.pallas_call(
        _dn_prep_kernel,
        grid=(b, DN_HEADS, ns),
        in_specs=[pl.BlockSpec((8, AB_COLS), lambda bi, h, s: (0, 0)),
                  pl.BlockSpec((1, seg, DN_HEAD_COLS), lambda bi, h, s: (bi, s, c0 + h)),
                  pl.BlockSpec((1, 16, DN_HEAD_COLS), lambda bi, h, s: (bi, jnp.maximum(s * hb16 - 1, 0), c0 + h)),
                  pl.BlockSpec((1, 16, DN_HEAD_COLS),
                               lambda bi, h, s: (bi, jnp.minimum((s + 1) * hb16, last16), c0 + h)),
                  pl.BlockSpec((1, seg, AB_COLS), lambda bi, h, s: (bi, s, 0)),
                  pl.BlockSpec((1, 8, DN_HEAD_COLS), lambda bi, h, s: (h, 0, 0))],
        out_specs=per_dir_specs + per_dir_specs,
        out_shape=[big, big, med, med, gls] * 2,
        scratch_shapes=[pltpu.VMEM((seg + 16, DN_HEAD_COLS), F32), pltpu.VMEM((seg, AB_COLS), F32)],
        compiler_params=_cparams(("parallel", "parallel", "parallel")),
    )(gate_params, proj3, proj3, proj3, ab3, conv_h)


SCAN_HB = 4


def _dn_scan_kernel(knf_ref, bnf_ref, af_ref, o0f_ref, glf_ref, knb_ref, bnb_ref, ab_ref, o0b_ref, glb_ref,
                    of_ref, ob_ref, s_scr, *, nc):
    c = DN_CHUNK

    @pl.when(pl.program_id(2) == 0)
    def _():
        s_scr[...] = jnp.zeros(s_scr.shape, F32)

    def chain(idx, ci, kn_ref, bn_ref, a_ref, o0_ref, gl_ref, o_ref, hh):
        sl = slice(hh * HEAD_DIM, (hh + 1) * HEAD_DIM)
        r2 = pl.multiple_of(ci * 2 * c, 2 * c)
        r1 = pl.multiple_of(ci * c, c)
        lhs = jnp.concatenate([kn_ref[0, pl.ds(r2, 2 * c), sl], a_ref[0, pl.ds(r1, c), sl]], axis=0)
        st = s_scr[idx]
        t1 = _dot(lhs, st.astype(BF16))
        gl = gl_ref[0, pl.ds(pl.multiple_of(ci * 8, 8), 8), sl][0:1, :]
        s_scr[idx] = gl * st - t1[:2 * c] + bn_ref[0, pl.ds(r2, 2 * c), sl].astype(F32)
        o_ref[0, pl.ds(r1, c), sl] = (t1[2 * c:] + o0_ref[0, pl.ds(r1, c), sl].astype(F32)).astype(o_ref.dtype)

    def body(ci, carry):
        cb = nc - 1 - ci
        for hh in range(SCAN_HB):
            chain(hh, ci, knf_ref, bnf_ref, af_ref, o0f_ref, glf_ref, of_ref, hh)
            chain(SCAN_HB + hh, cb, knb_ref, bnb_ref, ab_ref, o0b_ref, glb_ref, ob_ref, hh)
        return carry

    lax.fori_loop(0, nc, body, 0)


def _dn_scan(prep):
    knf, bnf, af, o0f, glf, knb, bnb, ab_, o0b, glb = prep
    b, seq, _ = af.shape
    seg = min(1024, seq)
    nc = seg // DN_CHUNK
    ns = seq // seg
    w = SCAN_HB * HEAD_DIM
    fwd = lambda bi, hb, s: (bi, s, hb)
    bwd = lambda bi, hb, s: (bi, ns - 1 - s, hb)

    def specs(im):
        return [pl.BlockSpec((1, 2 * seg, w), im), pl.BlockSpec((1, 2 * seg, w), im),
                pl.BlockSpec((1, seg, w), im), pl.BlockSpec((1, seg, w), im),
                pl.BlockSpec((1, nc * 8, w), im)]

    out = jax.ShapeDtypeStruct((b, seq, DN_WIDTH), BF16)
    return pl.pallas_call(
        functools.partial(_dn_scan_kernel, nc=nc),
        grid=(b, DN_HEADS // SCAN_HB, ns),
        in_specs=specs(fwd) + specs(bwd),
        out_specs=[pl.BlockSpec((1, seg, w), fwd), pl.BlockSpec((1, seg, w), bwd)],
        out_shape=[out, out],
        scratch_shapes=[pltpu.VMEM((2 * SCAN_HB, HEAD_DIM, HEAD_DIM), F32)],
        compiler_params=_cparams(("parallel", "parallel", "arbitrary")),
    )(knf, bnf, af, o0f, glf, knb, bnb, ab_, o0b, glb)


def _attn_kernel(q_ref, kp_ref, km_ref, kn_ref, vp_ref, vm_ref, vn_ref, o_ref, lse_ref, *, tq):
    i = pl.program_id(2)
    nq = pl.num_programs(2)
    scale = HEAD_DIM ** -0.5
    row_m = lax.broadcasted_iota(jnp.int32, (tq, tq), 0)
    col_m = lax.broadcasted_iota(jnp.int32, (tq, tq), 1)
    mask_m = jnp.abs(row_m - col_m) <= HALO
    row_h = lax.broadcasted_iota(jnp.int32, (tq, HALO), 0)
    col_h = lax.broadcasted_iota(jnp.int32, (tq, HALO), 1)
    mask_p = jnp.logical_and(row_h <= col_h, i > 0)
    mask_n = jnp.logical_and(row_h - col_h >= tq - HALO, i < nq - 1)
    lane = lax.broadcasted_iota(jnp.int32, (tq, HEAD_DIM), 1)
    lse_tile = jnp.zeros((tq, HEAD_DIM), F32)
    for hh in range(HPG):
        sl = slice(hh * HEAD_DIM, (hh + 1) * HEAD_DIM)
        q = q_ref[0, 0, :, sl]
        sp = jnp.where(mask_p, _dot_nt(q, kp_ref[0, 0, :, sl]) * scale, MASK_VALUE)
        sm = jnp.where(mask_m, _dot_nt(q, km_ref[0, 0, :, sl]) * scale, MASK_VALUE)
        sn = jnp.where(mask_n, _dot_nt(q, kn_ref[0, 0, :, sl]) * scale, MASK_VALUE)
        m = jnp.maximum(jnp.max(sm, axis=-1, keepdims=True),
                        jnp.maximum(jnp.max(sp, axis=-1, keepdims=True), jnp.max(sn, axis=-1, keepdims=True)))
        pp = jnp.exp(sp - m)
        pm = jnp.exp(sm - m)
        pn = jnp.exp(sn - m)
        den = (jnp.sum(pm, axis=-1, keepdims=True) + jnp.sum(pp, axis=-1, keepdims=True)
               + jnp.sum(pn, axis=-1, keepdims=True))
        o = (_dot(pm.astype(BF16), vm_ref[0, 0, :, sl]) + _dot(pp.astype(BF16), vp_ref[0, 0, :, sl])
             + _dot(pn.astype(BF16), vn_ref[0, 0, :, sl]))
        o_ref[0, 0, :, sl] = (o / den).astype(o_ref.dtype)
        lse_tile = jnp.where(lane == hh, m + jnp.log(den), lse_tile)
    lse_ref[0, 0] = lse_tile


def _attention_group(at4, cq, ck, cv):
    b, dil, m, _ = at4.shape
    tq = min(256, m)
    nq = m // tq
    hq = tq // HALO
    lastb = m // HALO - 1

    def main(cb):
        return pl.BlockSpec((1, 1, tq, ATTN_OUT), lambda bi, r, i: (bi, r, i, cb))

    def prev(cb):
        return pl.BlockSpec((1, 1, HALO, ATTN_OUT), lambda bi, r, i: (bi, r, jnp.maximum(i * hq - 1, 0), cb))

    def nxt(cb):
        return pl.BlockSpec((1, 1, HALO, ATTN_OUT), lambda bi, r, i: (bi, r, jnp.minimum((i + 1) * hq, lastb), cb))

    return pl.pallas_call(
        functools.partial(_attn_kernel, tq=tq),
        grid=(b, dil, nq),
        in_specs=[main(cq), prev(ck), main(ck), nxt(ck), prev(cv), main(cv), nxt(cv)],
        out_specs=[pl.BlockSpec((1, 1, tq, ATTN_OUT), lambda bi, r, i: (bi, r, i, 0)),
                   pl.BlockSpec((1, 1, tq, HEAD_DIM), lambda bi, r, i: (bi, r, i, 0))],
        out_shape=[jax.ShapeDtypeStruct((b, dil, m, ATTN_OUT), BF16),
                   jax.ShapeDtypeStruct((b, dil, m, HEAD_DIM), F32)],
        compiler_params=_cparams(("parallel", "parallel", "parallel")),
    )(at4, at4, at4, at4, at4, at4, at4)


MIX_TM = 256


def _mix_kernel(x_ref, mod_ref, of_ref, ob_ref, z_ref, gdn_ref, gat_ref,
                o1_ref, o2_ref, o3_ref, l1_ref, l2_ref, l3_ref,
                dnw_ref, wdn_ref, wat_ref, wout_ref, npost_ref, out_ref, a_scr, b_scr,
                o2_scr, o3_scr, l2_scr, l3_scr):
    tm = a_scr.shape[0]
    for src, lsrc, dst, ldst, gi in ((o2_ref, l2_ref, o2_scr, l2_scr, 1), (o3_ref, l3_ref, o3_scr, l3_scr, 2)):
        dil = ATTN_GROUPS[gi][1]
        for r in range(dil):
            for hh in range(HPG):
                dst[hh, pl.ds(r, tm // dil, stride=dil), :] = (
                    src[0, r, :, hh * HEAD_DIM:(hh + 1) * HEAD_DIM].astype(F32))
            ldst[pl.ds(r, tm // dil, stride=dil), :] = lsrc[0, r]
    dnw = dnw_ref[...]
    for h in range(DN_HEADS):
        sl = slice(h * HEAD_DIM, (h + 1) * HEAD_DIM)
        o = of_ref[:, sl].astype(F32) + ob_ref[:, sl].astype(F32)
        y = o * lax.rsqrt(jnp.mean(o * o, axis=-1, keepdims=True) + NORM_EPS) * dnw
        a_scr[:, sl] = (y * _silu(z_ref[:, sl].astype(F32))).astype(BF16)
    l1 = l1_ref[...]
    l2 = l2_scr[...]
    l3 = l3_scr[...]
    lm = jnp.maximum(l1, jnp.maximum(l2, l3))
    e1 = jnp.exp(l1 - lm)
    e2 = jnp.exp(l2 - lm)
    e3 = jnp.exp(l3 - lm)
    inv = 1.0 / (e1 + e2 + e3)
    w1 = e1 * inv
    w2 = e2 * inv
    w3 = e3 * inv
    for h in range(HPG):
        sl = slice(h * HEAD_DIM, (h + 1) * HEAD_DIM)
        b_scr[:, sl] = (w1[:, h:h + 1] * o1_ref[:, sl].astype(F32) + w2[:, h:h + 1] * o2_scr[h]
                        + w3[:, h:h + 1] * o3_scr[h]).astype(BF16)
    y_dn = _dot(a_scr[...], wdn_ref[...])
    y_at = _dot(b_scr[...], wat_ref[...])
    mixed_in = (jax.nn.sigmoid(gdn_ref[...].astype(F32)) * y_dn
                + jax.nn.sigmoid(gat_ref[...].astype(F32)) * y_at).astype(BF16)
    mixed = _dot(mixed_in, wout_ref[...])
    nrm = mixed * lax.rsqrt(jnp.mean(mixed * mixed, axis=-1, keepdims=True) + NORM_EPS) * npost_ref[...]
    out_ref[...] = x_ref[...] + mod_ref[0, 2:3, :] * nrm


def _mix(x2, mod3, proj2, o_f, o_b, at_o, at_lse, dn_norm_w, w_dn_out, w_at_out, w_out, norm_post, seq):
    rows = x2.shape[0]
    tm = MIX_TM
    nt = seq // tm
    rowblk = lambda w: pl.BlockSpec((tm, w), lambda i: (i, 0))
    const = lambda shape: pl.BlockSpec(shape, lambda i: (0, 0), pipeline_mode=pl.Buffered(1))
    d1, d2 = ATTN_GROUPS[1][1], ATTN_GROUPS[2][1]
    dilblk = lambda dil, w: pl.BlockSpec((1, dil, tm // dil, w), lambda i: (i // nt, 0, i % nt, 0))
    return pl.pallas_call(
        _mix_kernel,
        grid=(rows // tm,),
        in_specs=[rowblk(D_MODEL),
                  pl.BlockSpec((1, N_MOD, D_MODEL), lambda i: (i // nt, 0, 0)),
                  rowblk(DN_WIDTH), rowblk(DN_WIDTH),
                  pl.BlockSpec((tm, DN_WIDTH), lambda i: (i, COL_Z // DN_WIDTH)),
                  pl.BlockSpec((tm, D_MODEL), lambda i: (i, COL_MERGE // D_MODEL)),
                  pl.BlockSpec((tm, D_MODEL), lambda i: (i, COL_MERGE // D_MODEL + 1)),
                  rowblk(ATTN_OUT), dilblk(d1, ATTN_OUT), dilblk(d2, ATTN_OUT),
                  rowblk(HEAD_DIM), dilblk(d1, HEAD_DIM), dilblk(d2, HEAD_DIM),
                  const((1, HEAD_DIM)), const((DN_WIDTH, D_MODEL)), const((ATTN_OUT, D_MODEL)),
                  const((D_MODEL, D_MODEL)), const((1, D_MODEL))],
        out_specs=rowblk(D_MODEL),
        out_shape=jax.ShapeDtypeStruct((rows, D_MODEL), F32),
        scratch_shapes=[pltpu.VMEM((tm, DN_WIDTH), BF16), pltpu.VMEM((tm, ATTN_OUT), BF16),
                        pltpu.VMEM((HPG, tm, HEAD_DIM), F32), pltpu.VMEM((HPG, tm, HEAD_DIM), F32),
                        pltpu.VMEM((tm, HEAD_DIM), F32), pltpu.VMEM((tm, HEAD_DIM), F32)],
        compiler_params=_cparams(("parallel",)),
    )(x2, mod3, o_f, o_b, proj2, proj2, proj2, at_o[0].reshape(rows, ATTN_OUT), at_o[1], at_o[2],
      at_lse[0].reshape(rows, HEAD_DIM), at_lse[1], at_lse[2],
      dn_norm_w.reshape(1, HEAD_DIM), w_dn_out, w_at_out, w_out, norm_post.reshape(1, D_MODEL))


FFN_TM = 1024
FFN_TF = 512


def _ffn_kernel(x_ref, mod_ref, npre_ref, w1_ref, w2_ref, npost_ref, out_ref, h_scr):
    j = pl.program_id(1)

    @pl.when(j == 0)
    def _():
        x = x_ref[...]
        y = x * lax.rsqrt(jnp.mean(x * x, axis=-1, keepdims=True) + NORM_EPS) * npre_ref[...]
        h_scr[...] = (y * (1.0 + mod_ref[0, 4:5, :]) + mod_ref[0, 3:4, :]).astype(BF16)
        out_ref[...] = jnp.zeros(out_ref.shape, F32)

    t = jnp.maximum(_dot(h_scr[...], w1_ref[...]), 0.0)
    out_ref[...] += _dot((t * t).astype(BF16), w2_ref[...])

    @pl.when(j == pl.num_programs(1) - 1)
    def _():
        f = out_ref[...]
        nrm = f * lax.rsqrt(jnp.mean(f * f, axis=-1, keepdims=True) + NORM_EPS) * npost_ref[...]
        out_ref[...] = x_ref[...] + mod_ref[0, 5:6, :] * nrm


def _ffn(x2, mod3, norm_pre, w_ff1, w_ff2, norm_post, seq):
    rows = x2.shape[0]
    tm = min(FFN_TM, seq)
    nt = seq // tm
    return pl.pallas_call(
        _ffn_kernel,
        grid=(rows // tm, D_FF // FFN_TF),
        in_specs=[pl.BlockSpec((tm, D_MODEL), lambda i, j: (i, 0)),
                  pl.BlockSpec((1, N_MOD, D_MODEL), lambda i, j: (i // nt, 0, 0)),
                  pl.BlockSpec((1, D_MODEL), lambda i, j: (0, 0)),
                  pl.BlockSpec((D_MODEL, FFN_TF), lambda i, j: (0, j)),
                  pl.BlockSpec((FFN_TF, D_MODEL), lambda i, j: (j, 0)),
                  pl.BlockSpec((1, D_MODEL), lambda i, j: (0, 0))],
        out_specs=pl.BlockSpec((tm, D_MODEL), lambda i, j: (i, 0)),
        out_shape=jax.ShapeDtypeStruct((rows, D_MODEL), F32),
        scratch_shapes=[pltpu.VMEM((tm, D_MODEL), BF16)],
        compiler_params=_cparams(("parallel", "arbitrary")),
    )(x2, mod3, norm_pre.reshape(1, D_MODEL), w_ff1, w_ff2, norm_post.reshape(1, D_MODEL))


def _prep_in_weights(w_in, conv_w):
    o = 0
    dn_qkv = w_in[:, o:o + 3 * DN_WIDTH]; o += 3 * DN_WIDTH
    dn_z = w_in[:, o:o + DN_WIDTH]; o += DN_WIDTH
    dn_ab = w_in[:, o:o + 4 * DN_HEADS]; o += 4 * DN_HEADS
    at_qkv = w_in[:, o:o + 3 * ATTN_WIDTH]; o += 3 * ATTN_WIDTH
    merge = w_in[:, o:o + 2 * D_MODEL]
    dn_ph = dn_qkv.reshape(D_MODEL, 3, DN_HEADS, HEAD_DIM).transpose(0, 2, 1, 3).reshape(D_MODEL, 3 * DN_WIDTH)
    at_pg = (at_qkv.reshape(D_MODEL, 3, N_GROUPS, ATTN_OUT).transpose(0, 2, 1, 3)
             .reshape(D_MODEL, 3 * ATTN_WIDTH))
    at0 = at_pg[:, :AT_GROUP_COLS]
    w_main = jnp.concatenate([at_pg[:, AT_GROUP_COLS:], at0[:, :2 * ATTN_OUT], dn_z, merge, dn_ph,
                              at0[:, 2 * ATTN_OUT:]], axis=1).astype(BF16)
    w_ab = jnp.pad(dn_ab, ((0, 0), (0, AB_COLS - 4 * DN_HEADS))).astype(BF16)
    conv_h = conv_w.reshape(DN_CONV, 3, DN_HEADS, HEAD_DIM).transpose(2, 0, 1, 3).reshape(DN_HEADS, DN_CONV, DN_HEAD_COLS)
    conv_h = jnp.pad(conv_h, ((0, 0), (0, 8 - DN_CONV), (0, 0)))
    return w_main, w_ab, conv_h


def _rope_tables(seq):
    half = HEAD_DIM // 2
    inv_freq = ROPE_THETA ** (-jnp.arange(half, dtype=F32) / half)
    ang = jnp.arange(seq, dtype=F32)[:, None] * inv_freq[None, :]
    cos = jnp.cos(ang)
    sin = jnp.sin(ang)
    return jnp.concatenate([cos, cos], axis=1), jnp.concatenate([-sin, sin], axis=1)


def _group_forward(x, mod, wts):
    b, seq, _ = x.shape
    rows = b * seq
    x2 = x.reshape(rows, D_MODEL)
    mod3 = mod.reshape(b, N_MOD, D_MODEL)
    cos_t, sin_t = _rope_tables(seq)
    proj2, at1, at2, ab2 = _in_proj(x2, mod3, wts["norm_pre_mix"], wts["w_main"], wts["w_ab"], cos_t, sin_t, seq)
    proj3 = proj2.reshape(b, seq, W_PROJ)
    prep = _dn_prep(proj3, ab2.reshape(b, seq, AB_COLS), wts["conv_h"], wts["A_log"], wts["dt_bias"])
    o_f, o_b = _dn_scan(prep)
    at = [_attention_group(proj2.reshape(b, 1, seq, W_PROJ), COL_AT0_QK // ATTN_OUT, COL_AT0_QK // ATTN_OUT + 1,
                           COL_AT0_V // ATTN_OUT),
          _attention_group(at1, 0, 1, 2), _attention_group(at2, 0, 1, 2)]
    x1 = _mix(x2, mod3, proj2, o_f.reshape(rows, DN_WIDTH), o_b.reshape(rows, DN_WIDTH),
              [a[0] for a in at], [a[1] for a in at], wts["dn_norm_w"], wts["w_dn_out"], wts["w_at_out"],
              wts["w_out"], wts["norm_post_mix"], seq)
    y = _ffn(x1, mod3, wts["norm_pre_ffn"], wts["w_ff1"], wts["w_ff2"], wts["norm_post_ffn"], seq)
    return y.reshape(b, seq, D_MODEL)


def kernel(x_prompt, x_sample, c_prompt, c_sample, w_ada, b_ada, norm_pre_mix, norm_post_mix, norm_pre_ffn,
           norm_post_ffn, w_in, conv_w, A_log, dt_bias, dn_norm_w, w_dn_out, w_at_out, w_out, w_ff1, w_ff2):
    xs = (x_prompt, x_sample)
    nbp = c_prompt.shape[0]
    c_all = jnp.concatenate([c_prompt, c_sample], axis=0)
    for l in range(w_ada.shape[0]):
        w_main, w_ab, conv_h = _prep_in_weights(w_in[l], conv_w[l])
        wts = dict(norm_pre_mix=norm_pre_mix[l], norm_post_mix=norm_post_mix[l], norm_pre_ffn=norm_pre_ffn[l],
                   norm_post_ffn=norm_post_ffn[l], w_main=w_main, w_ab=w_ab, conv_h=conv_h, A_log=A_log[l],
                   dt_bias=dt_bias[l], dn_norm_w=dn_norm_w[l], w_dn_out=w_dn_out[l].astype(BF16),
                   w_at_out=w_at_out[l].astype(BF16), w_out=w_out[l].astype(BF16),
                   w_ff1=w_ff1[l].astype(BF16), w_ff2=w_ff2[l].astype(BF16))
        mod_all = _ada_mod(c_all, w_ada[l], b_ada[l])
        xs = (_group_forward(xs[0], mod_all[:nbp], wts), _group_forward(xs[1], mod_all[nbp:], wts))
    return xs
```

```python
import functools

import jax
import jax.numpy as jnp
from jax import lax
from jax.experimental import pallas as pl
from jax.experimental.pallas import tpu as pltpu

F32 = jnp.float32
BF16 = jnp.bfloat16

D_MODEL = 2048
HEAD_DIM = 128
DN_HEADS = 8
DN_WIDTH = DN_HEADS * HEAD_DIM
DN_CONV = 5
DN_CHUNK = 64
ATTN_GROUPS = ((128, 1), (512, 4), (2048, 16))
N_GROUPS = 3
HPG = 4
ATTN_HEADS = N_GROUPS * HPG
ATTN_WIDTH = ATTN_HEADS * HEAD_DIM
ATTN_OUT = HPG * HEAD_DIM
ROPE_THETA = 10000.0
D_FF = 4 * D_MODEL
N_MOD = 6
NORM_EPS = 1e-6
MASK_VALUE = -1e30

DN_HEAD_COLS = 3 * HEAD_DIM
AT_GROUP_COLS = 3 * ATTN_OUT
COL_AT0_QK = 0
COL_Z = COL_AT0_QK + 2 * ATTN_OUT
COL_MERGE = COL_Z + DN_WIDTH
COL_DN = COL_MERGE + 2 * D_MODEL
COL_AT0_V = COL_DN + 3 * DN_WIDTH
W_PROJ = COL_AT0_V + ATTN_OUT
AB_COLS = 128

PROJ_TM = 1024
PROJ_TN = 512
DIL_TILES = 2 * AT_GROUP_COLS // PROJ_TN
MAIN_TILES = W_PROJ // PROJ_TN
ROT_TILES = DIL_TILES + COL_Z // PROJ_TN
W_MAIN = (DIL_TILES + MAIN_TILES) * PROJ_TN
HALO = 64

VMEM_LIMIT = 56 * 1024 * 1024


def _cparams(sem):
    return pltpu.CompilerParams(dimension_semantics=sem, vmem_limit_bytes=VMEM_LIMIT)


def _dot(a, b):
    return jnp.dot(a, b, preferred_element_type=F32)


def _dot_nt(a, b):
    return lax.dot_general(a, b, (((1,), (1,)), ((), ())), preferred_element_type=F32)


def _dot_tn(a, b):
    return lax.dot_general(a, b, (((0,), (0,)), ((), ())), preferred_element_type=F32)


def _silu(x):
    return x * jax.nn.sigmoid(x)


def _mod_kernel(c_ref, w_ref, b_ref, o_ref):
    s = _silu(c_ref[...]).astype(BF16)
    o_ref[...] = _dot(s, w_ref[...].astype(BF16)) + b_ref[...]


def _ada_mod(c_all, w_ada, b_ada):
    nb, n = c_all.shape[0], w_ada.shape[1]
    tn = 512
    return pl.pallas_call(
        _mod_kernel,
        grid=(n // tn,),
        in_specs=[pl.BlockSpec((nb, D_MODEL), lambda j: (0, 0)),
                  pl.BlockSpec((D_MODEL, tn), lambda j: (0, j)),
                  pl.BlockSpec((1, tn), lambda j: (0, j))],
        out_specs=pl.BlockSpec((nb, tn), lambda j: (0, j)),
        out_shape=jax.ShapeDtypeStruct((nb, n), F32),
        compiler_params=_cparams(("arbitrary",)),
    )(c_all, w_ada, b_ada.reshape(1, n))


def _inproj_kernel(x_ref, mod_ref, nw_ref, w_ref, wab_ref, cos_ref, sin_ref, o_ref, o1_ref, o2_ref, ab_ref,
                   h_scr, acc_scr):
    j = pl.program_id(1)
    tm = acc_scr.shape[1]
    nslab = PROJ_TN // HEAD_DIM

    @pl.when(j == 0)
    def _():
        x = x_ref[...]
        ms = jnp.mean(x * x, axis=-1, keepdims=True)
        y = x * lax.rsqrt(ms + NORM_EPS) * nw_ref[...]
        h = y * (1.0 + mod_ref[0, 1:2, :]) + mod_ref[0, 0:1, :]
        hb = h.astype(BF16)
        h_scr[...] = hb
        ab_ref[...] = _dot(hb, wab_ref[...])

    acc = _dot(h_scr[...], w_ref[...])
    is_rot = jnp.logical_and(j < ROT_TILES, lax.rem(j, 3) != 2)
    is_dil = j < DIL_TILES

    def rotated(a):
        return a * cos_ref[...] + pltpu.roll(a, HEAD_DIM // 2, axis=1) * sin_ref[...]

    o_ref[...] = acc.astype(BF16)

    @pl.when(jnp.logical_and(jnp.logical_not(is_dil), is_rot))
    def _():
        for hh in range(nslab):
            sl = slice(hh * HEAD_DIM, (hh + 1) * HEAD_DIM)
            o_ref[:, sl] = rotated(acc[:, sl]).astype(BF16)

    @pl.when(jnp.logical_and(is_dil, is_rot))
    def _():
        for hh in range(nslab):
            acc_scr[hh] = rotated(acc[:, hh * HEAD_DIM:(hh + 1) * HEAD_DIM])

    @pl.when(jnp.logical_and(is_dil, jnp.logical_not(is_rot)))
    def _():
        for hh in range(nslab):
            acc_scr[hh] = acc[:, hh * HEAD_DIM:(hh + 1) * HEAD_DIM]

    for o_dil, gi in ((o1_ref, 1), (o2_ref, 2)):
        dil = ATTN_GROUPS[gi][1]
        t0 = 3 * (gi - 1)

        @pl.when(jnp.logical_and(j >= t0, j < t0 + 3))
        def _(o_dil=o_dil, dil=dil):
            for r in range(dil):
                for hh in range(nslab):
                    o_dil[0, r, :, hh * HEAD_DIM:(hh + 1) * HEAD_DIM] = (
                        acc_scr[hh, pl.ds(r, tm // dil, stride=dil), :].astype(BF16))


def _in_proj(x2, mod3, norm_w, w_main, w_ab, cos_t, sin_t, seq):
    rows = x2.shape[0]
    b = rows // seq
    tm = min(PROJ_TM, seq)
    nt = seq // tm
    d1, d2 = ATTN_GROUPS[1][1], ATTN_GROUPS[2][1]

    def dil_spec(dil, t0):
        return pl.BlockSpec((1, dil, tm // dil, PROJ_TN),
                            lambda i, j: (i // nt, 0, i % nt, jnp.clip(j - t0, 0, 2)))

    def rope_block(i, j):
        return (i % nt, 0)

    return pl.pallas_call(
        _inproj_kernel,
        grid=(rows // tm, W_MAIN // PROJ_TN),
        in_specs=[pl.BlockSpec((tm, D_MODEL), lambda i, j: (i, 0)),
                  pl.BlockSpec((1, N_MOD, D_MODEL), lambda i, j: (i // nt, 0, 0)),
                  pl.BlockSpec((1, D_MODEL), lambda i, j: (0, 0)),
                  pl.BlockSpec((D_MODEL, PROJ_TN), lambda i, j: (0, j)),
                  pl.BlockSpec((D_MODEL, AB_COLS), lambda i, j: (0, 0)),
                  pl.BlockSpec((tm, HEAD_DIM), rope_block), pl.BlockSpec((tm, HEAD_DIM), rope_block)],
        out_specs=[pl.BlockSpec((tm, PROJ_TN), lambda i, j: (i, jnp.maximum(j - DIL_TILES, 0))),
                   dil_spec(d1, 0), dil_spec(d2, 3),
                   pl.BlockSpec((tm, AB_COLS), lambda i, j: (i, 0))],
        out_shape=[jax.ShapeDtypeStruct((rows, W_PROJ), BF16),
                   jax.ShapeDtypeStruct((b, d1, seq // d1, AT_GROUP_COLS), BF16),
                   jax.ShapeDtypeStruct((b, d2, seq // d2, AT_GROUP_COLS), BF16),
                   jax.ShapeDtypeStruct((rows, AB_COLS), F32)],
        scratch_shapes=[pltpu.VMEM((tm, D_MODEL), BF16), pltpu.VMEM((PROJ_TN // HEAD_DIM, tm, HEAD_DIM), F32)],
        compiler_params=_cparams(("parallel", "arbitrary")),
    )(x2, mod3, norm_w.reshape(1, D_MODEL), w_main, w_ab, cos_t, sin_t)


PREP_SEG = 512
PREP_NC = PREP_SEG // DN_CHUNK
PREP_UNROLL = 8


def _each(f, *lists):
    return [f(*args) for args in zip(*lists)]


def _aligned(x, m):
    return x if isinstance(x, int) else pl.multiple_of(x, m)


def _tri_inverse_pairs(lms, eye2, pair_mask, level_masks, bd_mask):
    l_bd = _each(lambda lm: jnp.concatenate([lm, lm], axis=0), lms)
    xs = _each(lambda lm: eye2 - jnp.where(pair_mask, lm, 0.0), lms)
    for mask in level_masks:
        n_bd = _each(lambda lb: jnp.where(mask, lb, 0.0).astype(BF16), l_bd)
        ys = _each(lambda x, n: _dot(x.astype(BF16), n), xs, n_bd)
        x_bd = _each(lambda x: jnp.where(bd_mask, jnp.concatenate([x, x], axis=0), 0.0).astype(BF16), xs)
        xs = _each(lambda x, y, xb: x - _dot(y.astype(BF16), xb), xs, ys, x_bd)
    return xs


def _dn_prep_kernel(gp_ref, main_ref, prev_ref, next_ref, ab_ref, cw_ref,
                    knf_ref, bnf_ref, af_ref, o0f_ref, glf_ref,
                    knb_ref, bnb_ref, ab_out_ref, o0b_ref, glb_ref, xe_scr, gb_scr):
    h = pl.program_id(1)
    s = pl.program_id(2)
    ns = pl.num_programs(2)
    c = DN_CHUNK
    seg = PREP_SEG

    pv = prev_ref[0].astype(F32)[8:16, :] * (s > 0).astype(F32)
    nx = next_ref[0].astype(F32)[0:8, :] * (s < ns - 1).astype(F32)
    xe_scr[0:8, :] = pv
    xe_scr[8:seg + 8, :] = main_ref[0].astype(F32)
    xe_scr[seg + 8:seg + 16, :] = nx

    row = lax.broadcasted_iota(jnp.int32, (c, 2 * c), 0)
    lane = lax.broadcasted_iota(jnp.int32, (c, 2 * c), 1)
    col = jnp.bitwise_and(lane, c - 1)
    is_f = lane < c
    is_b = jnp.logical_not(is_f)
    incl2 = jnp.logical_or(jnp.logical_and(is_f, row >= col), jnp.logical_and(is_b, row <= col))
    strict2 = jnp.logical_or(jnp.logical_and(is_f, row > col), jnp.logical_and(is_b, row < col))
    eye2 = (row == col).astype(F32)
    tri_cat = ((col <= row).astype(BF16), (col >= row).astype(BF16))
    gr_mask = (jnp.logical_and(is_f, row <= col), jnp.logical_and(is_b, row >= col))
    ones4 = jnp.ones((c, 4 * c), BF16)
    row128 = lax.broadcasted_iota(jnp.int32, (2 * c, 2 * c), 0)
    lane128 = lax.broadcasted_iota(jnp.int32, (2 * c, 2 * c), 1)
    bd_mask = jnp.bitwise_and(row128, c) == jnp.bitwise_and(lane128, c)
    r64 = jnp.bitwise_and(row128, c - 1)
    c64 = jnp.bitwise_and(lane128, c - 1)

    def joins(ri, ci_, lg):
        return jnp.logical_and(jnp.right_shift(ri, lg) != jnp.right_shift(ci_, lg),
                               jnp.right_shift(ri, lg + 1) == jnp.right_shift(ci_, lg + 1))

    pair_mask = joins(row, col, 0)
    level_masks = [jnp.logical_and(bd_mask, joins(r64, c64, lg)) for lg in range(1, 6)]
    ab_all = ab_ref[0]
    z = ab_all + gp_ref[1:2, :]
    softplus = jnp.maximum(z, 0.0) + jnp.log(1.0 + jnp.exp(-jnp.abs(z)))
    lane_seg = lax.broadcasted_iota(jnp.int32, (seg, AB_COLS), 1)
    gb_scr[...] = jnp.where(lane_seg < 2 * DN_HEADS, -jnp.exp(gp_ref[0:1, :]) * softplus, jax.nn.sigmoid(ab_all))
    cw = cw_ref[0]
    outs = ((knf_ref, bnf_ref, af_ref, o0f_ref, glf_ref), (knb_ref, bnb_ref, ab_out_ref, o0b_ref, glb_ref))

    nb = PREP_UNROLL
    probs = [(j, d) for j in range(nb) for d in range(2)]
    pj = [j for j, _ in probs]
    pd = [d for _, d in probs]

    def body(it, carry):
        ci = [it * nb + j for j in range(nb)]
        r0 = [_aligned(cj * c, c) for cj in ci]
        xs = []
        for j in range(nb):
            if isinstance(r0[j], int):
                taps = [xe_scr[r0[j] + 6 + t:r0[j] + 6 + t + c, :] for t in range(DN_CONV)]
            else:
                win = xe_scr[pl.ds(r0[j], c + 16), :]
                taps = [win[6 + t:6 + t + c, :] for t in range(DN_CONV)]
            acc = cw[0:1, :] * taps[0]
            for t in range(1, DN_CONV):
                acc = acc + cw[t:t + 1, :] * taps[t]
            xs.append(_silu(acc))
        q = [x[:, :HEAD_DIM] for x in xs]
        k = [x[:, HEAD_DIM:2 * HEAD_DIM] for x in xs]
        v = [x[:, 2 * HEAD_DIM:] for x in xs]
        q = _each(lambda t: t * lax.rsqrt(jnp.sum(t * t, axis=-1, keepdims=True) + NORM_EPS) * (HEAD_DIM ** -0.5), q)
        k = _each(lambda t: t * lax.rsqrt(jnp.sum(t * t, axis=-1, keepdims=True) + NORM_EPS), k)
        qb = _each(lambda t: t.astype(BF16), q)
        kbf = _each(lambda t: t.astype(BF16), k)
        kq = _each(lambda kb_, qb_: _dot_nt(jnp.concatenate([kb_, qb_], axis=0), jnp.concatenate([kb_, kb_], axis=0)),
                   kbf, qb)
        gbc = [gb_scr[pl.ds(r0[j], c), :] for j in range(nb)]

        def gate_cols(j, d):
            g_col = jnp.sum(jnp.where(lane == d * DN_HEADS + h, gbc[j], 0.0), axis=-1, keepdims=True)
            b_col = jnp.sum(jnp.where(lane == 2 * DN_HEADS + d * DN_HEADS + h, gbc[j], 0.0), axis=-1, keepdims=True)
            return jnp.broadcast_to(g_col, (c, HEAD_DIM)), jnp.broadcast_to(b_col, (c, HEAD_DIM))

        gates = _each(gate_cols, pj, pd)
        g_b = [g for g, _ in gates]
        beta_b = [bt for _, bt in gates]
        g_hi = _each(lambda g: g.astype(BF16).astype(F32), g_b)
        g_lo = _each(lambda g, gh: g - gh, g_b, g_hi)
        gc = _each(lambda d, gh, gl_: _dot(tri_cat[d], jnp.concatenate([gh, gl_], axis=0).astype(BF16)),
                   pd, g_hi, g_lo)
        gtot = _each(lambda g: jnp.sum(g, axis=0, keepdims=True), g_b)
        exp_g = _each(jnp.exp, gc)

        def row_form(j):
            pf, pb = 2 * j, 2 * j + 1
            parts = [jnp.where(gr_mask[0], g_hi[pf], 0.0), jnp.where(gr_mask[0], g_lo[pf], 0.0),
                     jnp.where(gr_mask[1], g_hi[pb], 0.0), jnp.where(gr_mask[1], g_lo[pb], 0.0)]
            return _dot(ones4, jnp.concatenate(parts, axis=0).astype(BF16))

        gr2 = [row_form(j) for j in range(nb)]
        gc2 = [jnp.where(is_f, gc[2 * j], gc[2 * j + 1]) for j in range(nb)]
        beta2 = [jnp.where(is_f, beta_b[2 * j], beta_b[2 * j + 1]) for j in range(nb)]
        decay2 = _each(lambda a, b_: jnp.where(incl2, jnp.exp(jnp.where(incl2, a - b_, 0.0)), 0.0), gc2, gr2)
        lm2 = _each(lambda bt, kq_, dc: jnp.where(strict2, bt * kq_[:c] * dc, 0.0), beta2, kq, decay2)
        tmat2 = _tri_inverse_pairs(lm2, eye2, pair_mask, level_masks, bd_mask)
        qkm2 = _each(lambda kq_, dc: jnp.where(incl2, kq_[c:] * dc, 0.0), kq, decay2)
        half = lambda x2, d: x2[:, d * c:(d + 1) * c].astype(BF16)
        rhs = _each(lambda j, bt, eg: jnp.concatenate([k[j] * bt * eg, v[j] * bt], axis=1).astype(BF16),
                    pj, beta_b, exp_g)
        wu = _each(lambda j, d, r: _dot(half(tmat2[j], d), r).astype(BF16), pj, pd, rhs)
        ao = _each(lambda j, d, w_: _dot(half(qkm2[j], d), w_), pj, pd, wu)
        kd = _each(lambda j, gt, g: (k[j] * jnp.exp(gt - g)).astype(BF16), pj, gtot, gc)
        kbm = _each(_dot_tn, kd, wu)
        for p, (j, d) in enumerate(probs):
            kn_ref, bn_ref, a_ref, o0_ref, gl_ref = outs[d]
            r2 = _aligned(ci[j] * 2 * c, 2 * c)
            kn_ref[0, pl.ds(r2, 2 * c), :] = kbm[p][:, :HEAD_DIM].astype(BF16)
            bn_ref[0, pl.ds(r2, 2 * c), :] = kbm[p][:, HEAD_DIM:].astype(BF16)
            a_ref[0, pl.ds(r0[j], c), :] = (q[j] * exp_g[p] - ao[p][:, :HEAD_DIM]).astype(BF16)
            o0_ref[0, pl.ds(r0[j], c), :] = ao[p][:, HEAD_DIM:].astype(BF16)
            gl_ref[0, pl.ds(_aligned(ci[j] * 8, 8), 8), :] = jnp.broadcast_to(jnp.exp(gtot[p]), (8, HEAD_DIM))
        return carry

    if PREP_NC == nb:
        body(0, 0)
    else:
        lax.fori_loop(0, PREP_NC // nb, body, 0)


def _dn_prep(proj3, ab3, conv_h, a_log, dt_bias):
    b, seq, _ = proj3.shape
    seg = PREP_SEG
    ns = seq // seg
    hb16 = seg // 16
    last16 = seq // 16 - 1
    gate_params = jnp.zeros((8, AB_COLS), F32)
    gate_params = gate_params.at[0, :2 * DN_HEADS].set(a_log.reshape(-1)).at[1, :2 * DN_HEADS].set(dt_bias.reshape(-1))
    tok = lambda bi, h, s: (bi, s, h)
    c0 = COL_DN // DN_HEAD_COLS
    big = jax.ShapeDtypeStruct((b, 2 * seq, DN_WIDTH), BF16)
    med = jax.ShapeDtypeStruct((b, seq, DN_WIDTH), BF16)
    gls = jax.ShapeDtypeStruct((b, seq // DN_CHUNK * 8, DN_WIDTH), F32)
    per_dir_specs = [pl.BlockSpec((1, 2 * seg, HEAD_DIM), tok), pl.BlockSpec((1, 2 * seg, HEAD_DIM), tok),
                     pl.BlockSpec((1, seg, HEAD_DIM), tok), pl.BlockSpec((1, seg, HEAD_DIM), tok),
                     pl.BlockSpec((1, PREP_NC * 8, HEAD_DIM), tok)]
    return pl.pallas_call(
        _dn_prep_kernel,
        grid=(b, DN_HEADS, ns),
        in_specs=[pl.BlockSpec((8, AB_COLS), lambda bi, h, s: (0, 0)),
                  pl.BlockSpec((1, seg, DN_HEAD_COLS), lambda bi, h, s: (bi, s, c0 + h)),
                  pl.BlockSpec((1, 16, DN_HEAD_COLS), lambda bi, h, s: (bi, jnp.maximum(s * hb16 - 1, 0), c0 + h)),
                  pl.BlockSpec((1, 16, DN_HEAD_COLS),
                               lambda bi, h, s: (bi, jnp.minimum((s + 1) * hb16, last16), c0 + h)),
                  pl.BlockSpec((1, seg, AB_COLS), lambda bi, h, s: (bi, s, 0)),
                  pl.BlockSpec((1, 8, DN_HEAD_COLS), lambda bi, h, s: (h, 0, 0))],
        out_specs=per_dir_specs + per_dir_specs,
        out_shape=[big, big, med, med, gls] * 2,
        scratch_shapes=[pltpu.VMEM((seg + 16, DN_HEAD_COLS), F32), pltpu.VMEM((seg, AB_COLS), F32)],
        compiler_params=_cparams(("parallel", "parallel", "parallel")),
    )(gate_params, proj3, proj3, proj3, ab3, conv_h)


SCAN_HB = 4


def _dn_scan_kernel(knf_ref, bnf_ref, af_ref, o0f_ref, glf_ref, knb_ref, bnb_ref, ab_ref, o0b_ref, glb_ref,
                    of_ref, ob_ref, s_scr, *, nc):
    c = DN_CHUNK

    @pl.when(pl.program_id(2) == 0)
    def _():
        s_scr[...] = jnp.zeros(s_scr.shape, F32)

    def chain(idx, ci, kn_ref, bn_ref, a_ref, o0_ref, gl_ref, o_ref, hh):
        sl = slice(hh * HEAD_DIM, (hh + 1) * HEAD_DIM)
        r2 = pl.multiple_of(ci * 2 * c, 2 * c)
        r1 = pl.multiple_of(ci * c, c)
        lhs = jnp.concatenate([kn_ref[0, pl.ds(r2, 2 * c), sl], a_ref[0, pl.ds(r1, c), sl]], axis=0)
        st = s_scr[idx]
        t1 = _dot(lhs, st.astype(BF16))
        gl = gl_ref[0, pl.ds(pl.multiple_of(ci * 8, 8), 8), sl][0:1, :]
        s_scr[idx] = gl * st - t1[:2 * c] + bn_ref[0, pl.ds(r2, 2 * c), sl].astype(F32)
        o_ref[0, pl.ds(r1, c), sl] = (t1[2 * c:] + o0_ref[0, pl.ds(r1, c), sl].astype(F32)).astype(o_ref.dtype)

    def body(ci, carry):
        cb = nc - 1 - ci
        for hh in range(SCAN_HB):
            chain(hh, ci, knf_ref, bnf_ref, af_ref, o0f_ref, glf_ref, of_ref, hh)
            chain(SCAN_HB + hh, cb, knb_ref, bnb_ref, ab_ref, o0b_ref, glb_ref, ob_ref, hh)
        return carry

    lax.fori_loop(0, nc, body, 0)


def _dn_scan(prep):
    knf, bnf, af, o0f, glf, knb, bnb, ab_, o0b, glb = prep
    b, seq, _ = af.shape
    seg = min(1024, seq)
    nc = seg // DN_CHUNK
    ns = seq // seg
    w = SCAN_HB * HEAD_DIM
    fwd = lambda bi, hb, s: (bi, s, hb)
    bwd = lambda bi, hb, s: (bi, ns - 1 - s, hb)

    def specs(im):
        return [pl.BlockSpec((1, 2 * seg, w), im), pl.BlockSpec((1, 2 * seg, w), im),
                pl.BlockSpec((1, seg, w), im), pl.BlockSpec((1, seg, w), im),
                pl.BlockSpec((1, nc * 8, w), im)]

    out = jax.ShapeDtypeStruct((b, seq, DN_WIDTH), BF16)
    return pl.pallas_call(
        functools.partial(_dn_scan_kernel, nc=nc),
        grid=(b, DN_HEADS // SCAN_HB, ns),
        in_specs=specs(fwd) + specs(bwd),
        out_specs=[pl.BlockSpec((1, seg, w), fwd), pl.BlockSpec((1, seg, w), bwd)],
        out_shape=[out, out],
        scratch_shapes=[pltpu.VMEM((2 * SCAN_HB, HEAD_DIM, HEAD_DIM), F32)],
        compiler_params=_cparams(("parallel", "parallel", "arbitrary")),
    )(knf, bnf, af, o0f, glf, knb, bnb, ab_, o0b, glb)


ATTN_TQ = 512


def _attn_kernel(q_ref, kp_ref, km_ref, kn_ref, vp_ref, vm_ref, vn_ref, o_ref, lse_ref, *, tq):
    i = pl.program_id(2)
    nq = pl.num_programs(2)
    scale = HEAD_DIM ** -0.5
    win = tq + 2 * HALO
    row = lax.broadcasted_iota(jnp.int32, (tq, win), 0)
    colw = lax.broadcasted_iota(jnp.int32, (tq, win), 1)
    mask = jnp.abs(row - (colw - HALO)) <= HALO
    mask = jnp.logical_and(mask, jnp.logical_or(colw >= HALO, i > 0))
    mask = jnp.logical_and(mask, jnp.logical_or(colw < HALO + tq, i < nq - 1))
    lane = lax.broadcasted_iota(jnp.int32, (tq, HEAD_DIM), 1)
    for rb in range(q_ref.shape[1]):
        lse_tile = jnp.zeros((tq, HEAD_DIM), F32)
        for hh in range(HPG):
            sl = slice(hh * HEAD_DIM, (hh + 1) * HEAD_DIM)
            k_all = jnp.concatenate([kp_ref[0, rb, :, sl], km_ref[0, rb, :, sl], kn_ref[0, rb, :, sl]], axis=0)
            v_all = jnp.concatenate([vp_ref[0, rb, :, sl], vm_ref[0, rb, :, sl], vn_ref[0, rb, :, sl]], axis=0)
            s = jnp.where(mask, _dot_nt(q_ref[0, rb, :, sl], k_all) * scale, MASK_VALUE)
            m = jnp.max(s, axis=-1, keepdims=True)
            p = jnp.exp(s - m)
            den = jnp.sum(p, axis=-1, keepdims=True)
            o = _dot(p.astype(BF16), v_all)
            o_ref[0, rb, :, sl] = (o / den).astype(o_ref.dtype)
            lse_tile = jnp.where(lane == hh, m + jnp.log(den), lse_tile)
        lse_ref[0, rb] = lse_tile


def _attention_group(at4, cq, ck, cv):
    b, dil, m, _ = at4.shape
    tq = min(ATTN_TQ, m)
    rbk = min(dil, ATTN_TQ // tq)
    nq = m // tq
    hq = tq // HALO
    lastb = m // HALO - 1

    def main(cb):
        return pl.BlockSpec((1, rbk, tq, ATTN_OUT), lambda bi, r, i: (bi, r, i, cb))

    def prev(cb):
        return pl.BlockSpec((1, rbk, HALO, ATTN_OUT), lambda bi, r, i: (bi, r, jnp.maximum(i * hq - 1, 0), cb))

    def nxt(cb):
        return pl.BlockSpec((1, rbk, HALO, ATTN_OUT), lambda bi, r, i: (bi, r, jnp.minimum((i + 1) * hq, lastb), cb))

    return pl.pallas_call(
        functools.partial(_attn_kernel, tq=tq),
        grid=(b, dil // rbk, nq),
        in_specs=[main(cq), prev(ck), main(ck), nxt(ck), prev(cv), main(cv), nxt(cv)],
        out_specs=[pl.BlockSpec((1, rbk, tq, ATTN_OUT), lambda bi, r, i: (bi, r, i, 0)),
                   pl.BlockSpec((1, rbk, tq, HEAD_DIM), lambda bi, r, i: (bi, r, i, 0))],
        out_shape=[jax.ShapeDtypeStruct((b, dil, m, ATTN_OUT), BF16),
                   jax.ShapeDtypeStruct((b, dil, m, HEAD_DIM), F32)],
        compiler_params=_cparams(("parallel", "parallel", "parallel")),
    )(at4, at4, at4, at4, at4, at4, at4)


MIX_TM = 256


def _mix_kernel(x_ref, mod_ref, of_ref, ob_ref, z_ref, gdn_ref, gat_ref,
                o1_ref, o2_ref, o3_ref, l1_ref, l2_ref, l3_ref,
                dnw_ref, wdn_ref, wat_ref, wout_ref, npost_ref, out_ref, a_scr, b_scr,
                o2_scr, o3_scr, l2_scr, l3_scr):
    tm = a_scr.shape[0]
    for src, lsrc, dst, ldst, gi in ((o2_ref, l2_ref, o2_scr, l2_scr, 1), (o3_ref, l3_ref, o3_scr, l3_scr, 2)):
        dil = ATTN_GROUPS[gi][1]
        for r in range(dil):
            for hh in range(HPG):
                dst[hh, pl.ds(r, tm // dil, stride=dil), :] = (
                    src[0, r, :, hh * HEAD_DIM:(hh + 1) * HEAD_DIM].astype(F32))
            ldst[pl.ds(r, tm // dil, stride=dil), :] = lsrc[0, r]
    dnw = dnw_ref[...]
    for h in range(DN_HEADS):
        sl = slice(h * HEAD_DIM, (h + 1) * HEAD_DIM)
        o = of_ref[:, sl].astype(F32) + ob_ref[:, sl].astype(F32)
        y = o * lax.rsqrt(jnp.mean(o * o, axis=-1, keepdims=True) + NORM_EPS) * dnw
        a_scr[:, sl] = (y * _silu(z_ref[:, sl].astype(F32))).astype(BF16)
    l1 = l1_ref[...]
    l2 = l2_scr[...]
    l3 = l3_scr[...]
    lm = jnp.maximum(l1, jnp.maximum(l2, l3))
    e1 = jnp.exp(l1 - lm)
    e2 = jnp.exp(l2 - lm)
    e3 = jnp.exp(l3 - lm)
    inv = 1.0 / (e1 + e2 + e3)
    w1 = e1 * inv
    w2 = e2 * inv
    w3 = e3 * inv
    for h in range(HPG):
        sl = slice(h * HEAD_DIM, (h + 1) * HEAD_DIM)
        b_scr[:, sl] = (w1[:, h:h + 1] * o1_ref[:, sl].astype(F32) + w2[:, h:h + 1] * o2_scr[h]
                        + w3[:, h:h + 1] * o3_scr[h]).astype(BF16)
    y_dn = _dot(a_scr[...], wdn_ref[...])
    y_at = _dot(b_scr[...], wat_ref[...])
    mixed_in = (jax.nn.sigmoid(gdn_ref[...].astype(F32)) * y_dn
                + jax.nn.sigmoid(gat_ref[...].astype(F32)) * y_at).astype(BF16)
    mixed = _dot(mixed_in, wout_ref[...])
    nrm = mixed * lax.rsqrt(jnp.mean(mixed * mixed, axis=-1, keepdims=True) + NORM_EPS) * npost_ref[...]
    out_ref[...] = x_ref[...] + mod_ref[0, 2:3, :] * nrm


def _mix(x2, mod3, proj2, o_f, o_b, at_o, at_lse, dn_norm_w, w_dn_out, w_at_out, w_out, norm_post, seq):
    rows = x2.shape[0]
    tm = MIX_TM
    nt = seq // tm
    rowblk = lambda w: pl.BlockSpec((tm, w), lambda i: (i, 0))
    const = lambda shape: pl.BlockSpec(shape, lambda i: (0, 0), pipeline_mode=pl.Buffered(1))
    d1, d2 = ATTN_GROUPS[1][1], ATTN_GROUPS[2][1]
    dilblk = lambda dil, w: pl.BlockSpec((1, dil, tm // dil, w), lambda i: (i // nt, 0, i % nt, 0))
    return pl.pallas_call(
        _mix_kernel,
        grid=(rows // tm,),
        in_specs=[rowblk(D_MODEL),
                  pl.BlockSpec((1, N_MOD, D_MODEL), lambda i: (i // nt, 0, 0)),
                  rowblk(DN_WIDTH), rowblk(DN_WIDTH),
                  pl.BlockSpec((tm, DN_WIDTH), lambda i: (i, COL_Z // DN_WIDTH)),
                  pl.BlockSpec((tm, D_MODEL), lambda i: (i, COL_MERGE // D_MODEL)),
                  pl.BlockSpec((tm, D_MODEL), lambda i: (i, COL_MERGE // D_MODEL + 1)),
                  rowblk(ATTN_OUT), dilblk(d1, ATTN_OUT), dilblk(d2, ATTN_OUT),
                  rowblk(HEAD_DIM), dilblk(d1, HEAD_DIM), dilblk(d2, HEAD_DIM),
                  const((1, HEAD_DIM)), const((DN_WIDTH, D_MODEL)), const((ATTN_OUT, D_MODEL)),
                  const((D_MODEL, D_MODEL)), const((1, D_MODEL))],
        out_specs=rowblk(D_MODEL),
        out_shape=jax.ShapeDtypeStruct((rows, D_MODEL), F32),
        scratch_shapes=[pltpu.VMEM((tm, DN_WIDTH), BF16), pltpu.VMEM((tm, ATTN_OUT), BF16),
                        pltpu.VMEM((HPG, tm, HEAD_DIM), F32), pltpu.VMEM((HPG, tm, HEAD_DIM), F32),
                        pltpu.VMEM((tm, HEAD_DIM), F32), pltpu.VMEM((tm, HEAD_DIM), F32)],
        compiler_params=_cparams(("parallel",)),
    )(x2, mod3, o_f, o_b, proj2, proj2, proj2, at_o[0].reshape(rows, ATTN_OUT), at_o[1], at_o[2],
      at_lse[0].reshape(rows, HEAD_DIM), at_lse[1], at_lse[2],
      dn_norm_w.reshape(1, HEAD_DIM), w_dn_out, w_at_out, w_out, norm_post.reshape(1, D_MODEL))


FFN_TM = 1024
FFN_TF = 512


def _ffn_kernel(x_ref, mod_ref, npre_ref, w1_ref, w2_ref, npost_ref, out_ref, h_scr):
    j = pl.program_id(1)

    @pl.when(j == 0)
    def _():
        x = x_ref[...]
        y = x * lax.rsqrt(jnp.mean(x * x, axis=-1, keepdims=True) + NORM_EPS) * npre_ref[...]
        h_scr[...] = (y * (1.0 + mod_ref[0, 4:5, :]) + mod_ref[0, 3:4, :]).astype(BF16)
        out_ref[...] = jnp.zeros(out_ref.shape, F32)

    t = jnp.maximum(_dot(h_scr[...], w1_ref[...]), 0.0)
    out_ref[...] += _dot((t * t).astype(BF16), w2_ref[...])

    @pl.when(j == pl.num_programs(1) - 1)
    def _():
        f = out_ref[...]
        nrm = f * lax.rsqrt(jnp.mean(f * f, axis=-1, keepdims=True) + NORM_EPS) * npost_ref[...]
        out_ref[...] = x_ref[...] + mod_ref[0, 5:6, :] * nrm


def _ffn(x2, mod3, norm_pre, w_ff1, w_ff2, norm_post, seq):
    rows = x2.shape[0]
    tm = min(FFN_TM, seq)
    nt = seq // tm
    return pl.pallas_call(
        _ffn_kernel,
        grid=(rows // tm, D_FF // FFN_TF),
        in_specs=[pl.BlockSpec((tm, D_MODEL), lambda i, j: (i, 0)),
                  pl.BlockSpec((1, N_MOD, D_MODEL), lambda i, j: (i // nt, 0, 0)),
                  pl.BlockSpec((1, D_MODEL), lambda i, j: (0, 0)),
                  pl.BlockSpec((D_MODEL, FFN_TF), lambda i, j: (0, j)),
                  pl.BlockSpec((FFN_TF, D_MODEL), lambda i, j: (j, 0)),
                  pl.BlockSpec((1, D_MODEL), lambda i, j: (0, 0))],
        out_specs=pl.BlockSpec((tm, D_MODEL), lambda i, j: (i, 0)),
        out_shape=jax.ShapeDtypeStruct((rows, D_MODEL), F32),
        scratch_shapes=[pltpu.VMEM((tm, D_MODEL), BF16)],
        compiler_params=_cparams(("parallel", "arbitrary")),
    )(x2, mod3, norm_pre.reshape(1, D_MODEL), w_ff1, w_ff2, norm_post.reshape(1, D_MODEL))


def _prep_in_weights(w_in, conv_w):
    w_in = w_in.astype(BF16)
    o = 0
    dn_qkv = w_in[:, o:o + 3 * DN_WIDTH]; o += 3 * DN_WIDTH
    dn_z = w_in[:, o:o + DN_WIDTH]; o += DN_WIDTH
    dn_ab = w_in[:, o:o + 4 * DN_HEADS]; o += 4 * DN_HEADS
    at_qkv = w_in[:, o:o + 3 * ATTN_WIDTH]; o += 3 * ATTN_WIDTH
    merge = w_in[:, o:o + 2 * D_MODEL]
    dn_ph = dn_qkv.reshape(D_MODEL, 3, DN_HEADS, HEAD_DIM).transpose(0, 2, 1, 3).reshape(D_MODEL, 3 * DN_WIDTH)
    at_pg = (at_qkv.reshape(D_MODEL, 3, N_GROUPS, ATTN_OUT).transpose(0, 2, 1, 3)
             .reshape(D_MODEL, 3 * ATTN_WIDTH))
    at0 = at_pg[:, :AT_GROUP_COLS]
    w_main = jnp.concatenate([at_pg[:, AT_GROUP_COLS:], at0[:, :2 * ATTN_OUT], dn_z, merge, dn_ph,
                              at0[:, 2 * ATTN_OUT:]], axis=1).astype(BF16)
    w_ab = jnp.pad(dn_ab, ((0, 0), (0, AB_COLS - 4 * DN_HEADS))).astype(BF16)
    conv_h = conv_w.reshape(DN_CONV, 3, DN_HEADS, HEAD_DIM).transpose(2, 0, 1, 3).reshape(DN_HEADS, DN_CONV, DN_HEAD_COLS)
    conv_h = jnp.pad(conv_h, ((0, 0), (0, 8 - DN_CONV), (0, 0)))
    return w_main, w_ab, conv_h


def _rope_tables(seq):
    half = HEAD_DIM // 2
    inv_freq = ROPE_THETA ** (-jnp.arange(half, dtype=F32) / half)
    ang = jnp.arange(seq, dtype=F32)[:, None] * inv_freq[None, :]
    cos = jnp.cos(ang)
    sin = jnp.sin(ang)
    return jnp.concatenate([cos, cos], axis=1), jnp.concatenate([-sin, sin], axis=1)


def _group_forward(x, mod, wts):
    b, seq, _ = x.shape
    rows = b * seq
    x2 = x.reshape(rows, D_MODEL)
    mod3 = mod.reshape(b, N_MOD, D_MODEL)
    cos_t, sin_t = _rope_tables(seq)
    proj2, at1, at2, ab2 = _in_proj(x2, mod3, wts["norm_pre_mix"], wts["w_main"], wts["w_ab"], cos_t, sin_t, seq)
    proj3 = proj2.reshape(b, seq, W_PROJ)
    prep = _dn_prep(proj3, ab2.reshape(b, seq, AB_COLS), wts["conv_h"], wts["A_log"], wts["dt_bias"])
    o_f, o_b = _dn_scan(prep)
    at = [_attention_group(proj2.reshape(b, 1, seq, W_PROJ), COL_AT0_QK // ATTN_OUT, COL_AT0_QK // ATTN_OUT + 1,
                           COL_AT0_V // ATTN_OUT),
          _attention_group(at1, 0, 1, 2), _attention_group(at2, 0, 1, 2)]
    x1 = _mix(x2, mod3, proj2, o_f.reshape(rows, DN_WIDTH), o_b.reshape(rows, DN_WIDTH),
              [a[0] for a in at], [a[1] for a in at], wts["dn_norm_w"], wts["w_dn_out"], wts["w_at_out"],
              wts["w_out"], wts["norm_post_mix"], seq)
    y = _ffn(x1, mod3, wts["norm_pre_ffn"], wts["w_ff1"], wts["w_ff2"], wts["norm_post_ffn"], seq)
    return y.reshape(b, seq, D_MODEL)


def kernel(x_prompt, x_sample, c_prompt, c_sample, w_ada, b_ada, norm_pre_mix, norm_post_mix, norm_pre_ffn,
           norm_post_ffn, w_in, conv_w, A_log, dt_bias, dn_norm_w, w_dn_out, w_at_out, w_out, w_ff1, w_ff2):
    xs = (x_prompt, x_sample)
    nbp = c_prompt.shape[0]
    c_all = jnp.concatenate([c_prompt, c_sample], axis=0)
    for l in range(w_ada.shape[0]):
        w_main, w_ab, conv_h = _prep_in_weights(w_in[l], conv_w[l])
        wts = dict(norm_pre_mix=norm_pre_mix[l], norm_post_mix=norm_post_mix[l], norm_pre_ffn=norm_pre_ffn[l],
                   norm_post_ffn=norm_post_ffn[l], w_main=w_main, w_ab=w_ab, conv_h=conv_h, A_log=A_log[l],
                   dt_bias=dt_bias[l], dn_norm_w=dn_norm_w[l], w_dn_out=w_dn_out[l].astype(BF16),
                   w_at_out=w_at_out[l].astype(BF16), w_out=w_out[l].astype(BF16),
                   w_ff1=w_ff1[l].astype(BF16), w_ff2=w_ff2[l].astype(BF16))
        mod_all = _ada_mod(c_all, w_ada[l], b_ada[l])
        xs = (_group_forward(xs[0], mod_all[:nbp], wts), _group_forward(xs[1], mod_all[nbp:], wts))
    return xs
```

```python
import functools

import jax
import jax.numpy as jnp
from jax import lax
from jax.experimental import pallas as pl
from jax.experimental.pallas import tpu as pltpu

F32 = jnp.float32
BF16 = jnp.bfloat16

D_MODEL = 2048
HEAD_DIM = 128
DN_HEADS = 8
DN_WIDTH = DN_HEADS * HEAD_DIM
DN_CONV = 5
DN_CHUNK = 64
ATTN_GROUPS = ((128, 1), (512, 4), (2048, 16))
N_GROUPS = 3
HPG = 4
ATTN_HEADS = N_GROUPS * HPG
ATTN_WIDTH = ATTN_HEADS * HEAD_DIM
ATTN_OUT = HPG * HEAD_DIM
ROPE_THETA = 10000.0
D_FF = 4 * D_MODEL
N_MOD = 6
NORM_EPS = 1e-6
MASK_VALUE = -1e30

DN_HEAD_COLS = 3 * HEAD_DIM
AT_GROUP_COLS = 3 * ATTN_OUT
COL_AT0_QK = 0
COL_Z = COL_AT0_QK + 2 * ATTN_OUT
COL_MERGE = COL_Z + DN_WIDTH
COL_DN = COL_MERGE + 2 * D_MODEL
COL_AT0_V = COL_DN + 3 * DN_WIDTH
W_PROJ = COL_AT0_V + ATTN_OUT
AB_COLS = 128

PROJ_TM = 1024
PROJ_TN = 512
DIL_TILES = 2 * AT_GROUP_COLS // PROJ_TN
MAIN_TILES = W_PROJ // PROJ_TN
ROT_TILES = DIL_TILES + COL_Z // PROJ_TN
W_MAIN = (DIL_TILES + MAIN_TILES) * PROJ_TN
HALO = 64

VMEM_LIMIT = 56 * 1024 * 1024


def _cparams(sem):
    return pltpu.CompilerParams(dimension_semantics=sem, vmem_limit_bytes=VMEM_LIMIT)


def _dot(a, b):
    return jnp.dot(a, b, preferred_element_type=F32)


def _dot_nt(a, b):
    return lax.dot_general(a, b, (((1,), (1,)), ((), ())), preferred_element_type=F32)


def _dot_tn(a, b):
    return lax.dot_general(a, b, (((0,), (0,)), ((), ())), preferred_element_type=F32)


def _silu(x):
    return x * jax.nn.sigmoid(x)


def _mod_kernel(c_ref, w_ref, b_ref, o_ref):
    s = _silu(c_ref[...]).astype(BF16)
    o_ref[...] = _dot(s, w_ref[...].astype(BF16)) + b_ref[...]


def _ada_mod(c_all, w_ada, b_ada):
    nb, n = c_all.shape[0], w_ada.shape[1]
    tn = 512
    return pl.pallas_call(
        _mod_kernel,
        grid=(n // tn,),
        in_specs=[pl.BlockSpec((nb, D_MODEL), lambda j: (0, 0)),
                  pl.BlockSpec((D_MODEL, tn), lambda j: (0, j)),
                  pl.BlockSpec((1, tn), lambda j: (0, j))],
        out_specs=pl.BlockSpec((nb, tn), lambda j: (0, j)),
        out_shape=jax.ShapeDtypeStruct((nb, n), F32),
        compiler_params=_cparams(("arbitrary",)),
    )(c_all, w_ada, b_ada.reshape(1, n))


def _inproj_kernel(x_ref, mod_ref, nw_ref, w_ref, wab_ref, cos_ref, sin_ref, o_ref, o1_ref, o2_ref, ab_ref,
                   h_scr, acc_scr):
    j = pl.program_id(1)
    tm = acc_scr.shape[1]
    nslab = PROJ_TN // HEAD_DIM

    @pl.when(j == 0)
    def _():
        x = x_ref[...]
        ms = jnp.mean(x * x, axis=-1, keepdims=True)
        y = x * lax.rsqrt(ms + NORM_EPS) * nw_ref[...]
        h = y * (1.0 + mod_ref[0, 1:2, :]) + mod_ref[0, 0:1, :]
        hb = h.astype(BF16)
        h_scr[...] = hb
        ab_ref[...] = _dot(hb, wab_ref[...])

    acc = _dot(h_scr[...], w_ref[...])
    is_rot = jnp.logical_and(j < ROT_TILES, lax.rem(j, 3) != 2)
    is_dil = j < DIL_TILES

    def rotated(a):
        return a * cos_ref[...] + pltpu.roll(a, HEAD_DIM // 2, axis=1) * sin_ref[...]

    o_ref[...] = acc.astype(BF16)

    @pl.when(jnp.logical_and(jnp.logical_not(is_dil), is_rot))
    def _():
        for hh in range(nslab):
            sl = slice(hh * HEAD_DIM, (hh + 1) * HEAD_DIM)
            o_ref[:, sl] = rotated(acc[:, sl]).astype(BF16)

    @pl.when(jnp.logical_and(is_dil, is_rot))
    def _():
        for hh in range(nslab):
            acc_scr[hh] = rotated(acc[:, hh * HEAD_DIM:(hh + 1) * HEAD_DIM])

    @pl.when(jnp.logical_and(is_dil, jnp.logical_not(is_rot)))
    def _():
        for hh in range(nslab):
            acc_scr[hh] = acc[:, hh * HEAD_DIM:(hh + 1) * HEAD_DIM]

    for o_dil, gi in ((o1_ref, 1), (o2_ref, 2)):
        dil = ATTN_GROUPS[gi][1]
        t0 = 3 * (gi - 1)

        @pl.when(jnp.logical_and(j >= t0, j < t0 + 3))
        def _(o_dil=o_dil, dil=dil):
            for r in range(dil):
                for hh in range(nslab):
                    o_dil[0, r, :, hh * HEAD_DIM:(hh + 1) * HEAD_DIM] = (
                        acc_scr[hh, pl.ds(r, tm // dil, stride=dil), :].astype(BF16))


def _in_proj(x2, mod3, norm_w, w_main, w_ab, cos_t, sin_t, seq):
    rows = x2.shape[0]
    b = rows // seq
    tm = min(PROJ_TM, seq)
    nt = seq // tm
    d1, d2 = ATTN_GROUPS[1][1], ATTN_GROUPS[2][1]

    def dil_spec(dil, t0):
        return pl.BlockSpec((1, dil, tm // dil, PROJ_TN),
                            lambda i, j: (i // nt, 0, i % nt, jnp.clip(j - t0, 0, 2)))

    def rope_block(i, j):
        return (i % nt, 0)

    return pl.pallas_call(
        _inproj_kernel,
        grid=(rows // tm, W_MAIN // PROJ_TN),
        in_specs=[pl.BlockSpec((tm, D_MODEL), lambda i, j: (i, 0)),
                  pl.BlockSpec((1, N_MOD, D_MODEL), lambda i, j: (i // nt, 0, 0)),
                  pl.BlockSpec((1, D_MODEL), lambda i, j: (0, 0)),
                  pl.BlockSpec((D_MODEL, PROJ_TN), lambda i, j: (0, j)),
                  pl.BlockSpec((D_MODEL, AB_COLS), lambda i, j: (0, 0)),
                  pl.BlockSpec((tm, HEAD_DIM), rope_block), pl.BlockSpec((tm, HEAD_DIM), rope_block)],
        out_specs=[pl.BlockSpec((tm, PROJ_TN), lambda i, j: (i, jnp.maximum(j - DIL_TILES, 0))),
                   dil_spec(d1, 0), dil_spec(d2, 3),
                   pl.BlockSpec((tm, AB_COLS), lambda i, j: (i, 0))],
        out_shape=[jax.ShapeDtypeStruct((rows, W_PROJ), BF16),
                   jax.ShapeDtypeStruct((b, d1, seq // d1, AT_GROUP_COLS), BF16),
                   jax.ShapeDtypeStruct((b, d2, seq // d2, AT_GROUP_COLS), BF16),
                   jax.ShapeDtypeStruct((rows, AB_COLS), F32)],
        scratch_shapes=[pltpu.VMEM((tm, D_MODEL), BF16), pltpu.VMEM((PROJ_TN // HEAD_DIM, tm, HEAD_DIM), F32)],
        compiler_params=_cparams(("parallel", "arbitrary")),
    )(x2, mod3, norm_w.reshape(1, D_MODEL), w_main, w_ab, cos_t, sin_t)


PREP_SEG = 2048
PREP_NC = PREP_SEG // DN_CHUNK
PREP_SET = 4
PREP_FRONT_STAGES = 4


def _each(f, *lists):
    return [f(*args) for args in zip(*lists)]


def _aligned(x, m):
    return x if isinstance(x, int) else pl.multiple_of(x, m)


def _tri_inverse_pairs(lms, eye2, pair_mask, level_masks, bd_mask):
    l_bd = _each(lambda lm: jnp.concatenate([lm, lm], axis=0), lms)
    xs = _each(lambda lm: eye2 - jnp.where(pair_mask, lm, 0.0), lms)
    for mask in level_masks:
        n_bd = _each(lambda lb: jnp.where(mask, lb, 0.0).astype(BF16), l_bd)
        ys = _each(lambda x, n: _dot(x.astype(BF16), n), xs, n_bd)
        yield
        x_bd = _each(lambda x: jnp.where(bd_mask, jnp.concatenate([x, x], axis=0), 0.0).astype(BF16), xs)
        xs = _each(lambda x, y, xb: x - _dot(y.astype(BF16), xb), xs, ys, x_bd)
        yield
    return xs


def _dn_prep_kernel(gp_ref, main_ref, prev_ref, next_ref, ab_ref, cw_ref,
                    knf_ref, bnf_ref, af_ref, o0f_ref, glf_ref,
                    knb_ref, bnb_ref, ab_out_ref, o0b_ref, glb_ref, xe_scr, gb_scr):
    h = pl.program_id(1)
    s = pl.program_id(2)
    ns = pl.num_programs(2)
    c = DN_CHUNK
    seg = PREP_SEG

    xe_scr[0:16, :] = prev_ref[0] * (s > 0).astype(BF16)
    xe_scr[16:seg + 16, :] = main_ref[0]
    xe_scr[seg + 16:seg + 32, :] = next_ref[0] * (s < ns - 1).astype(BF16)
    srow = lax.broadcasted_iota(jnp.int32, (4 * c, c + 32), 0)
    scol = lax.broadcasted_iota(jnp.int32, (4 * c, c + 32), 1)
    stap = jnp.right_shift(srow, 6)
    stap = stap + (stap >= 2).astype(jnp.int32)
    shift_sel = (scol == jnp.bitwise_and(srow, c - 1) + 14 + stap).astype(BF16)

    row = lax.broadcasted_iota(jnp.int32, (c, 2 * c), 0)
    lane = lax.broadcasted_iota(jnp.int32, (c, 2 * c), 1)
    col = jnp.bitwise_and(lane, c - 1)
    is_f = lane < c
    is_b = jnp.logical_not(is_f)
    incl2 = jnp.logical_or(jnp.logical_and(is_f, row >= col), jnp.logical_and(is_b, row <= col))
    strict2 = jnp.logical_or(jnp.logical_and(is_f, row > col), jnp.logical_and(is_b, row < col))
    eye2 = (row == col).astype(F32)
    tri_cat = ((col <= row).astype(BF16), (col >= row).astype(BF16))
    gr_mask = (jnp.logical_and(is_f, row <= col), jnp.logical_and(is_b, row >= col))
    ones4 = jnp.ones((c, 4 * c), BF16)
    row128 = lax.broadcasted_iota(jnp.int32, (2 * c, 2 * c), 0)
    lane128 = lax.broadcasted_iota(jnp.int32, (2 * c, 2 * c), 1)
    bd_mask = jnp.bitwise_and(row128, c) == jnp.bitwise_and(lane128, c)
    r64 = jnp.bitwise_and(row128, c - 1)
    c64 = jnp.bitwise_and(lane128, c - 1)

    def joins(ri, ci_, lg):
        return jnp.logical_and(jnp.right_shift(ri, lg) != jnp.right_shift(ci_, lg),
                               jnp.right_shift(ri, lg + 1) == jnp.right_shift(ci_, lg + 1))

    pair_mask = joins(row, col, 0)
    level_masks = [jnp.logical_and(bd_mask, joins(r64, c64, lg)) for lg in range(1, 6)]
    ab_all = ab_ref[0]
    z = ab_all + gp_ref[1:2, :]
    softplus = jnp.maximum(z, 0.0) + jnp.log(1.0 + jnp.exp(-jnp.abs(z)))
    lane_seg = lax.broadcasted_iota(jnp.int32, (seg, AB_COLS), 1)
    gb_scr[...] = jnp.where(lane_seg < 2 * DN_HEADS, -jnp.exp(gp_ref[0:1, :]) * softplus, jax.nn.sigmoid(ab_all))
    cw = cw_ref[0]
    outs = ((knf_ref, bnf_ref, af_ref, o0f_ref, glf_ref), (knb_ref, bnb_ref, ab_out_ref, o0b_ref, glb_ref))

    nb = PREP_SET
    probs = [(j, d) for j in range(nb) for d in range(2)]
    pj = [j for j, _ in probs]
    pd = [d for _, d in probs]

    def chunk_set(ci):
        r0 = [cj * c for cj in ci]
        wins = [xe_scr[pl.ds(r0[j], c + 32), :] for j in range(nb)]
        shifted = _each(lambda w_: _dot(shift_sel, w_), wins)

        def conv_silu(w_, sh):
            acc = cw[2:3, :] * w_[16:16 + c, :].astype(F32)
            for ti, t in enumerate((0, 1, 3, 4)):
                acc = acc + cw[t:t + 1, :] * sh[ti * c:(ti + 1) * c, :]
            return _silu(acc)

        xs = _each(conv_silu, wins, shifted)
        yield
        q = [x[:, :HEAD_DIM] for x in xs]
        k = [x[:, HEAD_DIM:2 * HEAD_DIM] for x in xs]
        v = [x[:, 2 * HEAD_DIM:] for x in xs]
        q = _each(lambda t: t * lax.rsqrt(jnp.sum(t * t, axis=-1, keepdims=True) + NORM_EPS) * (HEAD_DIM ** -0.5), q)
        k = _each(lambda t: t * lax.rsqrt(jnp.sum(t * t, axis=-1, keepdims=True) + NORM_EPS), k)
        qb = _each(lambda t: t.astype(BF16), q)
        kbf = _each(lambda t: t.astype(BF16), k)
        kq = _each(lambda kb_, qb_: _dot_nt(jnp.concatenate([kb_, qb_], axis=0), jnp.concatenate([kb_, kb_], axis=0)),
                   kbf, qb)
        yield
        gbc = [gb_scr[pl.ds(r0[j], c), :] for j in range(nb)]

        def gate_cols(j, d):
            g_col = jnp.sum(jnp.where(lane == d * DN_HEADS + h, gbc[j], 0.0), axis=-1, keepdims=True)
            b_col = jnp.sum(jnp.where(lane == 2 * DN_HEADS + d * DN_HEADS + h, gbc[j], 0.0), axis=-1, keepdims=True)
            return jnp.broadcast_to(g_col, (c, HEAD_DIM)), jnp.broadcast_to(b_col, (c, HEAD_DIM))

        gates = _each(gate_cols, pj, pd)
        g_b = [g for g, _ in gates]
        beta_b = [bt for _, bt in gates]
        g_hi = _each(lambda g: g.astype(BF16).astype(F32), g_b)
        g_lo = _each(lambda g, gh: g - gh, g_b, g_hi)
        gc = _each(lambda d, gh, gl_: _dot(tri_cat[d], jnp.concatenate([gh, gl_], axis=0).astype(BF16)),
                   pd, g_hi, g_lo)
        gtot = _each(lambda g: jnp.sum(g, axis=0, keepdims=True), g_b)
        exp_g = _each(jnp.exp, gc)
        yield

        def row_form(j):
            pf, pb = 2 * j, 2 * j + 1
            parts = [jnp.where(gr_mask[0], g_hi[pf], 0.0), jnp.where(gr_mask[0], g_lo[pf], 0.0),
                     jnp.where(gr_mask[1], g_hi[pb], 0.0), jnp.where(gr_mask[1], g_lo[pb], 0.0)]
            return _dot(ones4, jnp.concatenate(parts, axis=0).astype(BF16))

        gr2 = [row_form(j) for j in range(nb)]
        gc2 = [jnp.where(is_f, gc[2 * j], gc[2 * j + 1]) for j in range(nb)]
        beta2 = [jnp.where(is_f, beta_b[2 * j], beta_b[2 * j + 1]) for j in range(nb)]
        decay2 = _each(lambda a, b_: jnp.where(incl2, jnp.exp(jnp.where(incl2, a - b_, 0.0)), 0.0), gc2, gr2)
        lm2 = _each(lambda bt, kq_, dc: jnp.where(strict2, bt * kq_[:c] * dc, 0.0), beta2, kq, decay2)
        yield
        tmat2 = yield from _tri_inverse_pairs(lm2, eye2, pair_mask, level_masks, bd_mask)
        qkm2 = _each(lambda kq_, dc: jnp.where(incl2, kq_[c:] * dc, 0.0), kq, decay2)
        half = lambda x2, d: x2[:, d * c:(d + 1) * c].astype(BF16)
        rhs = _each(lambda j, bt, eg: jnp.concatenate([k[j] * bt * eg, v[j] * bt], axis=1).astype(BF16),
                    pj, beta_b, exp_g)
        wu = _each(lambda j, d, r: _dot(half(tmat2[j], d), r).astype(BF16), pj, pd, rhs)
        yield
        ao = _each(lambda j, d, w_: _dot(half(qkm2[j], d), w_), pj, pd, wu)
        kd = _each(lambda j, gt, g: (k[j] * jnp.exp(gt - g)).astype(BF16), pj, gtot, gc)
        kbm = _each(_dot_tn, kd, wu)
        yield
        for p, (j, d) in enumerate(probs):
            kn_ref, bn_ref, a_ref, o0_ref, gl_ref = outs[d]
            kn_ref[0, pl.ds(2 * r0[j], 2 * c), :] = kbm[p][:, :HEAD_DIM].astype(BF16)
            bn_ref[0, pl.ds(2 * r0[j], 2 * c), :] = kbm[p][:, HEAD_DIM:].astype(BF16)
            a_ref[0, pl.ds(r0[j], c), :] = (q[j] * exp_g[p] - ao[p][:, :HEAD_DIM]).astype(BF16)
            o0_ref[0, pl.ds(r0[j], c), :] = ao[p][:, HEAD_DIM:].astype(BF16)
            gl_ref[0, pl.ds(ci[j] * 8, 8), :] = jnp.broadcast_to(jnp.exp(gtot[p]), (8, HEAD_DIM))

    pending = [chunk_set(list(range(s0, s0 + nb))) for s0 in range(0, PREP_NC, nb)]
    active = []
    while pending or active:
        if pending and (not active or active[-1][1] >= PREP_FRONT_STAGES):
            active.append([pending.pop(0), 0])
        for entry in list(active):
            try:
                next(entry[0])
                entry[1] += 1
            except StopIteration:
                active.remove(entry)


def _dn_prep(proj3, ab3, conv_h, a_log, dt_bias):
    b, seq, _ = proj3.shape
    seg = PREP_SEG
    ns = seq // seg
    hb16 = seg // 16
    last16 = seq // 16 - 1
    gate_params = jnp.zeros((8, AB_COLS), F32)
    gate_params = gate_params.at[0, :2 * DN_HEADS].set(a_log.reshape(-1)).at[1, :2 * DN_HEADS].set(dt_bias.reshape(-1))
    tok = lambda bi, h, s: (bi, s, h)
    c0 = COL_DN // DN_HEAD_COLS
    big = jax.ShapeDtypeStruct((b, 2 * seq, DN_WIDTH), BF16)
    med = jax.ShapeDtypeStruct((b, seq, DN_WIDTH), BF16)
    gls = jax.ShapeDtypeStruct((b, seq // DN_CHUNK * 8, DN_WIDTH), F32)
    per_dir_specs = [pl.BlockSpec((1, 2 * seg, HEAD_DIM), tok), pl.BlockSpec((1, 2 * seg, HEAD_DIM), tok),
                     pl.BlockSpec((1, seg, HEAD_DIM), tok), pl.BlockSpec((1, seg, HEAD_DIM), tok),
                     pl.BlockSpec((1, PREP_NC * 8, HEAD_DIM), tok)]
    return pl.pallas_call(
        _dn_prep_kernel,
        grid=(b, DN_HEADS, ns),
        in_specs=[pl.BlockSpec((8, AB_COLS), lambda bi, h, s: (0, 0)),
                  pl.BlockSpec((1, seg, DN_HEAD_COLS), lambda bi, h, s: (bi, s, c0 + h)),
                  pl.BlockSpec((1, 16, DN_HEAD_COLS), lambda bi, h, s: (bi, jnp.maximum(s * hb16 - 1, 0), c0 + h)),
                  pl.BlockSpec((1, 16, DN_HEAD_COLS),
                               lambda bi, h, s: (bi, jnp.minimum((s + 1) * hb16, last16), c0 + h)),
                  pl.BlockSpec((1, seg, AB_COLS), lambda bi, h, s: (bi, s, 0)),
                  pl.BlockSpec((1, 8, DN_HEAD_COLS), lambda bi, h, s: (h, 0, 0))],
        out_specs=per_dir_specs + per_dir_specs,
        out_shape=[big, big, med, med, gls] * 2,
        scratch_shapes=[pltpu.VMEM((seg + 32, DN_HEAD_COLS), BF16), pltpu.VMEM((seg, AB_COLS), F32)],
        compiler_params=_cparams(("parallel", "parallel", "parallel")),
    )(gate_params, proj3, proj3, proj3, ab3, conv_h)


SCAN_HB = 4


def _dn_scan_kernel(knf_ref, bnf_ref, af_ref, o0f_ref, glf_ref, knb_ref, bnb_ref, ab_ref, o0b_ref, glb_ref,
                    of_ref, ob_ref, s_scr, *, nc):
    c = DN_CHUNK

    @pl.when(pl.program_id(2) == 0)
    def _():
        s_scr[...] = jnp.zeros(s_scr.shape, F32)

    def chain(idx, ci, kn_ref, bn_ref, a_ref, o0_ref, gl_ref, o_ref, hh):
        sl = slice(hh * HEAD_DIM, (hh + 1) * HEAD_DIM)
        r2 = pl.multiple_of(ci * 2 * c, 2 * c)
        r1 = pl.multiple_of(ci * c, c)
        lhs = jnp.concatenate([kn_ref[0, pl.ds(r2, 2 * c), sl], a_ref[0, pl.ds(r1, c), sl]], axis=0)
        st = s_scr[idx]
        t1 = _dot(lhs, st.astype(BF16))
        gl = gl_ref[0, pl.ds(pl.multiple_of(ci * 8, 8), 8), sl][0:1, :]
        s_scr[idx] = gl * st - t1[:2 * c] + bn_ref[0, pl.ds(r2, 2 * c), sl].astype(F32)
        o_ref[0, pl.ds(r1, c), sl] = (t1[2 * c:] + o0_ref[0, pl.ds(r1, c), sl].astype(F32)).astype(o_ref.dtype)

    def body(ci, carry):
        cb = nc - 1 - ci
        for hh in range(SCAN_HB):
            chain(hh, ci, knf_ref, bnf_ref, af_ref, o0f_ref, glf_ref, of_ref, hh)
            chain(SCAN_HB + hh, cb, knb_ref, bnb_ref, ab_ref, o0b_ref, glb_ref, ob_ref, hh)
        return carry

    lax.fori_loop(0, nc, body, 0)


def _dn_scan(prep):
    knf, bnf, af, o0f, glf, knb, bnb, ab_, o0b, glb = prep
    b, seq, _ = af.shape
    seg = min(1024, seq)
    nc = seg // DN_CHUNK
    ns = seq // seg
    w = SCAN_HB * HEAD_DIM
    fwd = lambda bi, hb, s: (bi, s, hb)
    bwd = lambda bi, hb, s: (bi, ns - 1 - s, hb)

    def specs(im):
        return [pl.BlockSpec((1, 2 * seg, w), im), pl.BlockSpec((1, 2 * seg, w), im),
                pl.BlockSpec((1, seg, w), im), pl.BlockSpec((1, seg, w), im),
                pl.BlockSpec((1, nc * 8, w), im)]

    out = jax.ShapeDtypeStruct((b, seq, DN_WIDTH), BF16)
    return pl.pallas_call(
        functools.partial(_dn_scan_kernel, nc=nc),
        grid=(b, DN_HEADS // SCAN_HB, ns),
        in_specs=specs(fwd) + specs(bwd),
        out_specs=[pl.BlockSpec((1, seg, w), fwd), pl.BlockSpec((1, seg, w), bwd)],
        out_shape=[out, out],
        scratch_shapes=[pltpu.VMEM((2 * SCAN_HB, HEAD_DIM, HEAD_DIM), F32)],
        compiler_params=_cparams(("parallel", "parallel", "arbitrary")),
    )(knf, bnf, af, o0f, glf, knb, bnb, ab_, o0b, glb)


ATTN_TQ = 512


def _attn_kernel(q_ref, kp_ref, km_ref, kn_ref, vp_ref, vm_ref, vn_ref, o_ref, lse_ref, *, tq):
    i = pl.program_id(2)
    nq = pl.num_programs(2)
    scale = HEAD_DIM ** -0.5
    win = tq + 2 * HALO
    row = lax.broadcasted_iota(jnp.int32, (tq, win), 0)
    colw = lax.broadcasted_iota(jnp.int32, (tq, win), 1)
    mask = jnp.abs(row - (colw - HALO)) <= HALO
    mask = jnp.logical_and(mask, jnp.logical_or(colw >= HALO, i > 0))
    mask = jnp.logical_and(mask, jnp.logical_or(colw < HALO + tq, i < nq - 1))
    lane = lax.broadcasted_iota(jnp.int32, (tq, HEAD_DIM), 1)
    for rb in range(q_ref.shape[1]):
        lse_tile = jnp.zeros((tq, HEAD_DIM), F32)
        for hh in range(HPG):
            sl = slice(hh * HEAD_DIM, (hh + 1) * HEAD_DIM)
            k_all = jnp.concatenate([kp_ref[0, rb, :, sl], km_ref[0, rb, :, sl], kn_ref[0, rb, :, sl]], axis=0)
            v_all = jnp.concatenate([vp_ref[0, rb, :, sl], vm_ref[0, rb, :, sl], vn_ref[0, rb, :, sl]], axis=0)
            s = jnp.where(mask, _dot_nt(q_ref[0, rb, :, sl], k_all) * scale, MASK_VALUE)
            m = jnp.max(s, axis=-1, keepdims=True)
            p = jnp.exp(s - m)
            den = jnp.sum(p, axis=-1, keepdims=True)
            o = _dot(p.astype(BF16), v_all)
            o_ref[0, rb, :, sl] = (o / den).astype(o_ref.dtype)
            lse_tile = jnp.where(lane == hh, m + jnp.log(den), lse_tile)
        lse_ref[0, rb] = lse_tile


def _attention_group(at4, cq, ck, cv):
    b, dil, m, _ = at4.shape
    tq = min(ATTN_TQ, m)
    rbk = min(dil, ATTN_TQ // tq)
    nq = m // tq
    hq = tq // HALO
    lastb = m // HALO - 1

    def main(cb):
        return pl.BlockSpec((1, rbk, tq, ATTN_OUT), lambda bi, r, i: (bi, r, i, cb))

    def prev(cb):
        return pl.BlockSpec((1, rbk, HALO, ATTN_OUT), lambda bi, r, i: (bi, r, jnp.maximum(i * hq - 1, 0), cb))

    def nxt(cb):
        return pl.BlockSpec((1, rbk, HALO, ATTN_OUT), lambda bi, r, i: (bi, r, jnp.minimum((i + 1) * hq, lastb), cb))

    return pl.pallas_call(
        functools.partial(_attn_kernel, tq=tq),
        grid=(b, dil // rbk, nq),
        in_specs=[main(cq), prev(ck), main(ck), nxt(ck), prev(cv), main(cv), nxt(cv)],
        out_specs=[pl.BlockSpec((1, rbk, tq, ATTN_OUT), lambda bi, r, i: (bi, r, i, 0)),
                   pl.BlockSpec((1, rbk, tq, HEAD_DIM), lambda bi, r, i: (bi, r, i, 0))],
        out_shape=[jax.ShapeDtypeStruct((b, dil, m, ATTN_OUT), BF16),
                   jax.ShapeDtypeStruct((b, dil, m, HEAD_DIM), F32)],
        compiler_params=_cparams(("parallel", "parallel", "parallel")),
    )(at4, at4, at4, at4, at4, at4, at4)


MIX_TM = 256


def _mix_kernel(x_ref, mod_ref, of_ref, ob_ref, z_ref, gdn_ref, gat_ref,
                o1_ref, o2_ref, o3_ref, l1_ref, l2_ref, l3_ref,
                dnw_ref, wdn_ref, wat_ref, wout_ref, npost_ref, out_ref, a_scr, b_scr,
                o2_scr, o3_scr, l2_scr, l3_scr):
    tm = a_scr.shape[0]
    for src, lsrc, dst, ldst, gi in ((o2_ref, l2_ref, o2_scr, l2_scr, 1), (o3_ref, l3_ref, o3_scr, l3_scr, 2)):
        dil = ATTN_GROUPS[gi][1]
        for r in range(dil):
            for hh in range(HPG):
                dst[hh, pl.ds(r, tm // dil, stride=dil), :] = (
                    src[0, r, :, hh * HEAD_DIM:(hh + 1) * HEAD_DIM].astype(F32))
            ldst[pl.ds(r, tm // dil, stride=dil), :] = lsrc[0, r]
    dnw = dnw_ref[...]
    for h in range(DN_HEADS):
        sl = slice(h * HEAD_DIM, (h + 1) * HEAD_DIM)
        o = of_ref[:, sl].astype(F32) + ob_ref[:, sl].astype(F32)
        y = o * lax.rsqrt(jnp.mean(o * o, axis=-1, keepdims=True) + NORM_EPS) * dnw
        a_scr[:, sl] = (y * _silu(z_ref[:, sl].astype(F32))).astype(BF16)
    l1 = l1_ref[...]
    l2 = l2_scr[...]
    l3 = l3_scr[...]
    lm = jnp.maximum(l1, jnp.maximum(l2, l3))
    e1 = jnp.exp(l1 - lm)
    e2 = jnp.exp(l2 - lm)
    e3 = jnp.exp(l3 - lm)
    inv = 1.0 / (e1 + e2 + e3)
    w1 = e1 * inv
    w2 = e2 * inv
    w3 = e3 * inv
    for h in range(HPG):
        sl = slice(h * HEAD_DIM, (h + 1) * HEAD_DIM)
        b_scr[:, sl] = (w1[:, h:h + 1] * o1_ref[:, sl].astype(F32) + w2[:, h:h + 1] * o2_scr[h]
                        + w3[:, h:h + 1] * o3_scr[h]).astype(BF16)
    y_dn = _dot(a_scr[...], wdn_ref[...])
    y_at = _dot(b_scr[...], wat_ref[...])
    mixed_in = (jax.nn.sigmoid(gdn_ref[...].astype(F32)) * y_dn
                + jax.nn.sigmoid(gat_ref[...].astype(F32)) * y_at).astype(BF16)
    mixed = _dot(mixed_in, wout_ref[...])
    nrm = mixed * lax.rsqrt(jnp.mean(mixed * mixed, axis=-1, keepdims=True) + NORM_EPS) * npost_ref[...]
    out_ref[...] = x_ref[...] + mod_ref[0, 2:3, :] * nrm


def _mix(x2, mod3, proj2, o_f, o_b, at_o, at_lse, dn_norm_w, w_dn_out, w_at_out, w_out, norm_post, seq):
    rows = x2.shape[0]
    tm = MIX_TM
    nt = seq // tm
    rowblk = lambda w: pl.BlockSpec((tm, w), lambda i: (i, 0))
    const = lambda shape: pl.BlockSpec(shape, lambda i: (0, 0), pipeline_mode=pl.Buffered(1))
    d1, d2 = ATTN_GROUPS[1][1], ATTN_GROUPS[2][1]
    dilblk = lambda dil, w: pl.BlockSpec((1, dil, tm // dil, w), lambda i: (i // nt, 0, i % nt, 0))
    return pl.pallas_call(
        _mix_kernel,
        grid=(rows // tm,),
        in_specs=[rowblk(D_MODEL),
                  pl.BlockSpec((1, N_MOD, D_MODEL), lambda i: (i // nt, 0, 0)),
                  rowblk(DN_WIDTH), rowblk(DN_WIDTH),
                  pl.BlockSpec((tm, DN_WIDTH), lambda i: (i, COL_Z // DN_WIDTH)),
                  pl.BlockSpec((tm, D_MODEL), lambda i: (i, COL_MERGE // D_MODEL)),
                  pl.BlockSpec((tm, D_MODEL), lambda i: (i, COL_MERGE // D_MODEL + 1)),
                  rowblk(ATTN_OUT), dilblk(d1, ATTN_OUT), dilblk(d2, ATTN_OUT),
                  rowblk(HEAD_DIM), dilblk(d1, HEAD_DIM), dilblk(d2, HEAD_DIM),
                  const((1, HEAD_DIM)), const((DN_WIDTH, D_MODEL)), const((ATTN_OUT, D_MODEL)),
                  const((D_MODEL, D_MODEL)), const((1, D_MODEL))],
        out_specs=rowblk(D_MODEL),
        out_shape=jax.ShapeDtypeStruct((rows, D_MODEL), F32),
        scratch_shapes=[pltpu.VMEM((tm, DN_WIDTH), BF16), pltpu.VMEM((tm, ATTN_OUT), BF16),
                        pltpu.VMEM((HPG, tm, HEAD_DIM), F32), pltpu.VMEM((HPG, tm, HEAD_DIM), F32),
                        pltpu.VMEM((tm, HEAD_DIM), F32), pltpu.VMEM((tm, HEAD_DIM), F32)],
        compiler_params=_cparams(("parallel",)),
    )(x2, mod3, o_f, o_b, proj2, proj2, proj2, at_o[0].reshape(rows, ATTN_OUT), at_o[1], at_o[2],
      at_lse[0].reshape(rows, HEAD_DIM), at_lse[1], at_lse[2],
      dn_norm_w.reshape(1, HEAD_DIM), w_dn_out, w_at_out, w_out, norm_post.reshape(1, D_MODEL))


FFN_TM = 1024
FFN_TF = 512


def _ffn_kernel(x_ref, mod_ref, npre_ref, w1_ref, w2_ref, npost_ref, out_ref, h_scr):
    j = pl.program_id(1)

    @pl.when(j == 0)
    def _():
        x = x_ref[...]
        y = x * lax.rsqrt(jnp.mean(x * x, axis=-1, keepdims=True) + NORM_EPS) * npre_ref[...]
        h_scr[...] = (y * (1.0 + mod_ref[0, 4:5, :]) + mod_ref[0, 3:4, :]).astype(BF16)
        out_ref[...] = jnp.zeros(out_ref.shape, F32)

    t = jnp.maximum(_dot(h_scr[...], w1_ref[...]), 0.0)
    out_ref[...] += _dot((t * t).astype(BF16), w2_ref[...])

    @pl.when(j == pl.num_programs(1) - 1)
    def _():
        f = out_ref[...]
        nrm = f * lax.rsqrt(jnp.mean(f * f, axis=-1, keepdims=True) + NORM_EPS) * npost_ref[...]
        out_ref[...] = x_ref[...] + mod_ref[0, 5:6, :] * nrm


def _ffn(x2, mod3, norm_pre, w_ff1, w_ff2, norm_post, seq):
    rows = x2.shape[0]
    tm = min(FFN_TM, seq)
    nt = seq // tm
    return pl.pallas_call(
        _ffn_kernel,
        grid=(rows // tm, D_FF // FFN_TF),
        in_specs=[pl.BlockSpec((tm, D_MODEL), lambda i, j: (i, 0)),
                  pl.BlockSpec((1, N_MOD, D_MODEL), lambda i, j: (i // nt, 0, 0)),
                  pl.BlockSpec((1, D_MODEL), lambda i, j: (0, 0)),
                  pl.BlockSpec((D_MODEL, FFN_TF), lambda i, j: (0, j)),
                  pl.BlockSpec((FFN_TF, D_MODEL), lambda i, j: (j, 0)),
                  pl.BlockSpec((1, D_MODEL), lambda i, j: (0, 0))],
        out_specs=pl.BlockSpec((tm, D_MODEL), lambda i, j: (i, 0)),
        out_shape=jax.ShapeDtypeStruct((rows, D_MODEL), F32),
        scratch_shapes=[pltpu.VMEM((tm, D_MODEL), BF16)],
        compiler_params=_cparams(("parallel", "arbitrary")),
    )(x2, mod3, norm_pre.reshape(1, D_MODEL), w_ff1, w_ff2, norm_post.reshape(1, D_MODEL))


def _prep_in_weights(w_in, conv_w):
    w_in = w_in.astype(BF16)
    o = 0
    dn_qkv = w_in[:, o:o + 3 * DN_WIDTH]; o += 3 * DN_WIDTH
    dn_z = w_in[:, o:o + DN_WIDTH]; o += DN_WIDTH
    dn_ab = w_in[:, o:o + 4 * DN_HEADS]; o += 4 * DN_HEADS
    at_qkv = w_in[:, o:o + 3 * ATTN_WIDTH]; o += 3 * ATTN_WIDTH
    merge = w_in[:, o:o + 2 * D_MODEL]
    dn_ph = dn_qkv.reshape(D_MODEL, 3, DN_HEADS, HEAD_DIM).transpose(0, 2, 1, 3).reshape(D_MODEL, 3 * DN_WIDTH)
    at_pg = (at_qkv.reshape(D_MODEL, 3, N_GROUPS, ATTN_OUT).transpose(0, 2, 1, 3)
             .reshape(D_MODEL, 3 * ATTN_WIDTH))
    at0 = at_pg[:, :AT_GROUP_COLS]
    w_main = jnp.concatenate([at_pg[:, AT_GROUP_COLS:], at0[:, :2 * ATTN_OUT], dn_z, merge, dn_ph,
                              at0[:, 2 * ATTN_OUT:]], axis=1).astype(BF16)
    w_ab = jnp.pad(dn_ab, ((0, 0), (0, AB_COLS - 4 * DN_HEADS))).astype(BF16)
    conv_h = conv_w.reshape(DN_CONV, 3, DN_HEADS, HEAD_DIM).transpose(2, 0, 1, 3).reshape(DN_HEADS, DN_CONV, DN_HEAD_COLS)
    conv_h = jnp.pad(conv_h, ((0, 0), (0, 8 - DN_CONV), (0, 0)))
    return w_main, w_ab, conv_h


def _rope_tables(seq):
    half = HEAD_DIM // 2
    inv_freq = ROPE_THETA ** (-jnp.arange(half, dtype=F32) / half)
    ang = jnp.arange(seq, dtype=F32)[:, None] * inv_freq[None, :]
    cos = jnp.cos(ang)
    sin = jnp.sin(ang)
    return jnp.concatenate([cos, cos], axis=1), jnp.concatenate([-sin, sin], axis=1)


def _group_forward(x, mod, wts):
    b, seq, _ = x.shape
    rows = b * seq
    x2 = x.reshape(rows, D_MODEL)
    mod3 = mod.reshape(b, N_MOD, D_MODEL)
    cos_t, sin_t = _rope_tables(seq)
    proj2, at1, at2, ab2 = _in_proj(x2, mod3, wts["norm_pre_mix"], wts["w_main"], wts["w_ab"], cos_t, sin_t, seq)
    proj3 = proj2.reshape(b, seq, W_PROJ)
    prep = _dn_prep(proj3, ab2.reshape(b, seq, AB_COLS), wts["conv_h"], wts["A_log"], wts["dt_bias"])
    o_f, o_b = _dn_scan(prep)
    at = [_attention_group(proj2.reshape(b, 1, seq, W_PROJ), COL_AT0_QK // ATTN_OUT, COL_AT0_QK // ATTN_OUT + 1,
                           COL_AT0_V // ATTN_OUT),
          _attention_group(at1, 0, 1, 2), _attention_group(at2, 0, 1, 2)]
    x1 = _mix(x2, mod3, proj2, o_f.reshape(rows, DN_WIDTH), o_b.reshape(rows, DN_WIDTH),
              [a[0] for a in at], [a[1] for a in at], wts["dn_norm_w"], wts["w_dn_out"], wts["w_at_out"],
              wts["w_out"], wts["norm_post_mix"], seq)
    y = _ffn(x1, mod3, wts["norm_pre_ffn"], wts["w_ff1"], wts["w_ff2"], wts["norm_post_ffn"], seq)
    return y.reshape(b, seq, D_MODEL)


def kernel(x_prompt, x_sample, c_prompt, c_sample, w_ada, b_ada, norm_pre_mix, norm_post_mix, norm_pre_ffn,
           norm_post_ffn, w_in, conv_w, A_log, dt_bias, dn_norm_w, w_dn_out, w_at_out, w_out, w_ff1, w_ff2):
    xs = (x_prompt, x_sample)
    nbp = c_prompt.shape[0]
    c_all = jnp.concatenate([c_prompt, c_sample], axis=0)
    for l in range(w_ada.shape[0]):
        w_main, w_ab, conv_h = _prep_in_weights(w_in[l], conv_w[l])
        wts = dict(norm_pre_mix=norm_pre_mix[l], norm_post_mix=norm_post_mix[l], norm_pre_ffn=norm_pre_ffn[l],
                   norm_post_ffn=norm_post_ffn[l], w_main=w_main, w_ab=w_ab, conv_h=conv_h, A_log=A_log[l],
                   dt_bias=dt_bias[l], dn_norm_w=dn_norm_w[l], w_dn_out=w_dn_out[l].astype(BF16),
                   w_at_out=w_at_out[l].astype(BF16), w_out=w_out[l].astype(BF16),
                   w_ff1=w_ff1[l].astype(BF16), w_ff2=w_ff2[l].astype(BF16))
        mod_all = _ada_mod(c_all, w_ada[l], b_ada[l])
        xs = (_group_forward(xs[0], mod_all[:nbp], wts), _group_forward(xs[1], mod_all[nbp:], wts))
    return xs
```

```python
import functools

import jax
import jax.numpy as jnp
from jax import lax
from jax.experimental import pallas as pl
from jax.experimental.pallas import tpu as pltpu

F32 = jnp.float32
BF16 = jnp.bfloat16

D_MODEL = 2048
HEAD_DIM = 128
DN_HEADS = 8
DN_WIDTH = DN_HEADS * HEAD_DIM
DN_CONV = 5
DN_CHUNK = 64
ATTN_GROUPS = ((128, 1), (512, 4), (2048, 16))
N_GROUPS = 3
HPG = 4
ATTN_HEADS = N_GROUPS * HPG
ATTN_WIDTH = ATTN_HEADS * HEAD_DIM
ATTN_OUT = HPG * HEAD_DIM
ROPE_THETA = 10000.0
D_FF = 4 * D_MODEL
N_MOD = 6
NORM_EPS = 1e-6
MASK_VALUE = -1e30

DN_HEAD_COLS = 3 * HEAD_DIM
AT_GROUP_COLS = 3 * ATTN_OUT
COL_AT0_QK = 0
COL_Z = COL_AT0_QK + 2 * ATTN_OUT
COL_MERGE = COL_Z + DN_WIDTH
COL_DN = COL_MERGE + 2 * D_MODEL
COL_AT0_V = COL_DN + 3 * DN_WIDTH
W_PROJ = COL_AT0_V + ATTN_OUT
AB_COLS = 128

PROJ_TM = 1024
PROJ_TN = 512
DIL_TILES = 2 * AT_GROUP_COLS // PROJ_TN
MAIN_TILES = W_PROJ // PROJ_TN
ROT_TILES = DIL_TILES + COL_Z // PROJ_TN
W_MAIN = (DIL_TILES + MAIN_TILES) * PROJ_TN
HALO = 64

VMEM_LIMIT = 56 * 1024 * 1024


def _cparams(sem, vmem_limit=VMEM_LIMIT):
    return pltpu.CompilerParams(dimension_semantics=sem, vmem_limit_bytes=vmem_limit)


def _dot(a, b):
    return jnp.dot(a, b, preferred_element_type=F32)


def _dot_nt(a, b):
    return lax.dot_general(a, b, (((1,), (1,)), ((), ())), preferred_element_type=F32)


def _dot_tn(a, b):
    return lax.dot_general(a, b, (((0,), (0,)), ((), ())), preferred_element_type=F32)


def _silu(x):
    return x * jax.nn.sigmoid(x)


def _mod_kernel(c_ref, w_ref, b_ref, o_ref):
    s = _silu(c_ref[...]).astype(BF16)
    o_ref[...] = _dot(s, w_ref[...].astype(BF16)) + b_ref[...]


def _ada_mod(c_all, w_ada, b_ada):
    nb, n = c_all.shape[0], w_ada.shape[1]
    tn = 512
    return pl.pallas_call(
        _mod_kernel,
        grid=(n // tn,),
        in_specs=[pl.BlockSpec((nb, D_MODEL), lambda j: (0, 0)),
                  pl.BlockSpec((D_MODEL, tn), lambda j: (0, j)),
                  pl.BlockSpec((1, tn), lambda j: (0, j))],
        out_specs=pl.BlockSpec((nb, tn), lambda j: (0, j)),
        out_shape=jax.ShapeDtypeStruct((nb, n), F32),
        compiler_params=_cparams(("arbitrary",)),
    )(c_all, w_ada, b_ada.reshape(1, n))


def _inproj_kernel(x_ref, mod_ref, nw_ref, w_ref, wab_ref, cos_ref, sin_ref, o_ref, o1_ref, o2_ref, ab_ref,
                   h_scr, acc_scr):
    j = pl.program_id(1)
    tm = acc_scr.shape[1]
    nslab = PROJ_TN // HEAD_DIM

    @pl.when(j == 0)
    def _():
        x = x_ref[...]
        ms = jnp.mean(x * x, axis=-1, keepdims=True)
        y = x * lax.rsqrt(ms + NORM_EPS) * nw_ref[...]
        h = y * (1.0 + mod_ref[0, 1:2, :]) + mod_ref[0, 0:1, :]
        hb = h.astype(BF16)
        h_scr[...] = hb
        ab_ref[...] = _dot(hb, wab_ref[...])

    acc = _dot(h_scr[...], w_ref[...])
    is_rot = jnp.logical_and(j < ROT_TILES, lax.rem(j, 3) != 2)
    is_dil = j < DIL_TILES

    def rotated(a):
        return a * cos_ref[...] + pltpu.roll(a, HEAD_DIM // 2, axis=1) * sin_ref[...]

    o_ref[...] = acc.astype(BF16)

    @pl.when(jnp.logical_and(jnp.logical_not(is_dil), is_rot))
    def _():
        for hh in range(nslab):
            sl = slice(hh * HEAD_DIM, (hh + 1) * HEAD_DIM)
            o_ref[:, sl] = rotated(acc[:, sl]).astype(BF16)

    @pl.when(jnp.logical_and(is_dil, is_rot))
    def _():
        for hh in range(nslab):
            acc_scr[hh] = rotated(acc[:, hh * HEAD_DIM:(hh + 1) * HEAD_DIM])

    @pl.when(jnp.logical_and(is_dil, jnp.logical_not(is_rot)))
    def _():
        for hh in range(nslab):
            acc_scr[hh] = acc[:, hh * HEAD_DIM:(hh + 1) * HEAD_DIM]

    for o_dil, gi in ((o1_ref, 1), (o2_ref, 2)):
        dil = ATTN_GROUPS[gi][1]
        t0 = 3 * (gi - 1)

        @pl.when(jnp.logical_and(j >= t0, j < t0 + 3))
        def _(o_dil=o_dil, dil=dil):
            for r in range(dil):
                for hh in range(nslab):
                    o_dil[0, r, :, hh * HEAD_DIM:(hh + 1) * HEAD_DIM] = (
                        acc_scr[hh, pl.ds(r, tm // dil, stride=dil), :].astype(BF16))


def _in_proj(x2, mod3, norm_w, w_main, w_ab, cos_t, sin_t, seq):
    rows = x2.shape[0]
    b = rows // seq
    tm = min(PROJ_TM, seq)
    nt = seq // tm
    d1, d2 = ATTN_GROUPS[1][1], ATTN_GROUPS[2][1]

    def dil_spec(dil, t0):
        return pl.BlockSpec((1, dil, tm // dil, PROJ_TN),
                            lambda i, j: (i // nt, 0, i % nt, jnp.clip(j - t0, 0, 2)))

    def rope_block(i, j):
        return (i % nt, 0)

    return pl.pallas_call(
        _inproj_kernel,
        grid=(rows // tm, W_MAIN // PROJ_TN),
        in_specs=[pl.BlockSpec((tm, D_MODEL), lambda i, j: (i, 0)),
                  pl.BlockSpec((1, N_MOD, D_MODEL), lambda i, j: (i // nt, 0, 0)),
                  pl.BlockSpec((1, D_MODEL), lambda i, j: (0, 0)),
                  pl.BlockSpec((D_MODEL, PROJ_TN), lambda i, j: (0, j)),
                  pl.BlockSpec((D_MODEL, AB_COLS), lambda i, j: (0, 0)),
                  pl.BlockSpec((tm, HEAD_DIM), rope_block), pl.BlockSpec((tm, HEAD_DIM), rope_block)],
        out_specs=[pl.BlockSpec((tm, PROJ_TN), lambda i, j: (i, jnp.maximum(j - DIL_TILES, 0))),
                   dil_spec(d1, 0), dil_spec(d2, 3),
                   pl.BlockSpec((tm, AB_COLS), lambda i, j: (i, 0))],
        out_shape=[jax.ShapeDtypeStruct((rows, W_PROJ), BF16),
                   jax.ShapeDtypeStruct((b, d1, seq // d1, AT_GROUP_COLS), BF16),
                   jax.ShapeDtypeStruct((b, d2, seq // d2, AT_GROUP_COLS), BF16),
                   jax.ShapeDtypeStruct((rows, AB_COLS), F32)],
        scratch_shapes=[pltpu.VMEM((tm, D_MODEL), BF16), pltpu.VMEM((PROJ_TN // HEAD_DIM, tm, HEAD_DIM), F32)],
        compiler_params=_cparams(("parallel", "arbitrary")),
    )(x2, mod3, norm_w.reshape(1, D_MODEL), w_main, w_ab, cos_t, sin_t)


PREP_SEG = 2048
PREP_NC = PREP_SEG // DN_CHUNK
PREP_SET = 4
PREP_START_GAP = 2


def _each(f, *lists):
    return [f(*args) for args in zip(*lists)]


def _aligned(x, m):
    return x if isinstance(x, int) else pl.multiple_of(x, m)


def _tri_inverse_pairs(lms, eye2, pair_mask, level_masks, bd_mask):
    l_bd = _each(lambda lm: jnp.concatenate([lm, lm], axis=0), lms)
    xs = _each(lambda lm: eye2 - jnp.where(pair_mask, lm, 0.0), lms)
    for mask in level_masks:
        n_bd = _each(lambda lb: jnp.where(mask, lb, 0.0).astype(BF16), l_bd)
        ys = _each(lambda x, n: _dot(x.astype(BF16), n), xs, n_bd)
        yield
        x_bd = _each(lambda x: jnp.where(bd_mask, jnp.concatenate([x, x], axis=0), 0.0).astype(BF16), xs)
        xs = _each(lambda x, y, xb: x - _dot(y.astype(BF16), xb), xs, ys, x_bd)
        yield
    return xs


def _dn_prep_kernel(gp_ref, main_ref, prev_ref, next_ref, ab_ref, cw_ref,
                    knf_ref, bnf_ref, af_ref, o0f_ref, glf_ref,
                    knb_ref, bnb_ref, ab_out_ref, o0b_ref, glb_ref, xe_scr, gb_scr):
    h = pl.program_id(1)
    s = pl.program_id(2)
    ns = pl.num_programs(2)
    c = DN_CHUNK
    seg = PREP_SEG

    xe_scr[0:16, :] = prev_ref[0] * (s > 0).astype(BF16)
    xe_scr[16:seg + 16, :] = main_ref[0]
    xe_scr[seg + 16:seg + 32, :] = next_ref[0] * (s < ns - 1).astype(BF16)
    srow = lax.broadcasted_iota(jnp.int32, (4 * c, c + 32), 0)
    scol = lax.broadcasted_iota(jnp.int32, (4 * c, c + 32), 1)
    stap = jnp.right_shift(srow, 6)
    stap = stap + (stap >= 2).astype(jnp.int32)
    shift_sel = (scol == jnp.bitwise_and(srow, c - 1) + 14 + stap).astype(BF16)

    row = lax.broadcasted_iota(jnp.int32, (c, 2 * c), 0)
    lane = lax.broadcasted_iota(jnp.int32, (c, 2 * c), 1)
    col = jnp.bitwise_and(lane, c - 1)
    is_f = lane < c
    is_b = jnp.logical_not(is_f)
    incl2 = jnp.logical_or(jnp.logical_and(is_f, row >= col), jnp.logical_and(is_b, row <= col))
    strict2 = jnp.logical_or(jnp.logical_and(is_f, row > col), jnp.logical_and(is_b, row < col))
    eye2 = (row == col).astype(F32)
    tri_cat = ((col <= row).astype(BF16), (col >= row).astype(BF16))
    gr_mask = (jnp.logical_and(is_f, row <= col), jnp.logical_and(is_b, row >= col))
    ones4 = jnp.ones((c, 4 * c), BF16)
    row128 = lax.broadcasted_iota(jnp.int32, (2 * c, 2 * c), 0)
    lane128 = lax.broadcasted_iota(jnp.int32, (2 * c, 2 * c), 1)
    bd_mask = jnp.bitwise_and(row128, c) == jnp.bitwise_and(lane128, c)
    r64 = jnp.bitwise_and(row128, c - 1)
    c64 = jnp.bitwise_and(lane128, c - 1)

    def joins(ri, ci_, lg):
        return jnp.logical_and(jnp.right_shift(ri, lg) != jnp.right_shift(ci_, lg),
                               jnp.right_shift(ri, lg + 1) == jnp.right_shift(ci_, lg + 1))

    pair_mask = joins(row, col, 0)
    level_masks = [jnp.logical_and(bd_mask, joins(r64, c64, lg)) for lg in range(1, 6)]
    ab_all = ab_ref[0]
    z = ab_all + gp_ref[1:2, :]
    softplus = jnp.maximum(z, 0.0) + jnp.log(1.0 + jnp.exp(-jnp.abs(z)))
    lane_seg = lax.broadcasted_iota(jnp.int32, (seg, AB_COLS), 1)
    gb_scr[...] = jnp.where(lane_seg < 2 * DN_HEADS, -jnp.exp(gp_ref[0:1, :]) * softplus, jax.nn.sigmoid(ab_all))
    cw = cw_ref[0]
    outs = ((knf_ref, bnf_ref, af_ref, o0f_ref, glf_ref), (knb_ref, bnb_ref, ab_out_ref, o0b_ref, glb_ref))

    nb = PREP_SET
    probs = [(j, d) for j in range(nb) for d in range(2)]
    pj = [j for j, _ in probs]
    pd = [d for _, d in probs]

    def chunk_set(ci):
        r0 = [cj * c for cj in ci]
        wins = [xe_scr[pl.ds(r0[j], c + 32), :] for j in range(nb)]
        shifted = _each(lambda w_: _dot(shift_sel, w_), wins)

        def conv_silu(w_, sh):
            acc = cw[2:3, :] * w_[16:16 + c, :].astype(F32)
            for ti, t in enumerate((0, 1, 3, 4)):
                acc = acc + cw[t:t + 1, :] * sh[ti * c:(ti + 1) * c, :]
            return _silu(acc)

        xs = _each(conv_silu, wins, shifted)
        yield
        q = [x[:, :HEAD_DIM] for x in xs]
        k = [x[:, HEAD_DIM:2 * HEAD_DIM] for x in xs]
        v = [x[:, 2 * HEAD_DIM:] for x in xs]
        q = _each(lambda t: t * lax.rsqrt(jnp.sum(t * t, axis=-1, keepdims=True) + NORM_EPS) * (HEAD_DIM ** -0.5), q)
        k = _each(lambda t: t * lax.rsqrt(jnp.sum(t * t, axis=-1, keepdims=True) + NORM_EPS), k)
        qb = _each(lambda t: t.astype(BF16), q)
        kbf = _each(lambda t: t.astype(BF16), k)
        kq = _each(lambda kb_, qb_: _dot_nt(jnp.concatenate([kb_, qb_], axis=0), jnp.concatenate([kb_, kb_], axis=0)),
                   kbf, qb)
        yield
        gbc = [gb_scr[pl.ds(r0[j], c), :] for j in range(nb)]

        def gate_cols(j, d):
            g_col = jnp.sum(jnp.where(lane == d * DN_HEADS + h, gbc[j], 0.0), axis=-1, keepdims=True)
            b_col = jnp.sum(jnp.where(lane == 2 * DN_HEADS + d * DN_HEADS + h, gbc[j], 0.0), axis=-1, keepdims=True)
            return jnp.broadcast_to(g_col, (c, HEAD_DIM)), jnp.broadcast_to(b_col, (c, HEAD_DIM))

        gates = _each(gate_cols, pj, pd)
        g_b = [g for g, _ in gates]
        beta_b = [bt for _, bt in gates]
        g_hi = _each(lambda g: g.astype(BF16).astype(F32), g_b)
        g_lo = _each(lambda g, gh: g - gh, g_b, g_hi)
        gc = _each(lambda d, gh, gl_: _dot(tri_cat[d], jnp.concatenate([gh, gl_], axis=0).astype(BF16)),
                   pd, g_hi, g_lo)
        gtot = _each(lambda g: jnp.sum(g, axis=0, keepdims=True), g_b)
        exp_g = _each(jnp.exp, gc)
        yield

        def row_form(j):
            pf, pb = 2 * j, 2 * j + 1
            parts = [jnp.where(gr_mask[0], g_hi[pf], 0.0), jnp.where(gr_mask[0], g_lo[pf], 0.0),
                     jnp.where(gr_mask[1], g_hi[pb], 0.0), jnp.where(gr_mask[1], g_lo[pb], 0.0)]
            return _dot(ones4, jnp.concatenate(parts, axis=0).astype(BF16))

        gr2 = [row_form(j) for j in range(nb)]
        gc2 = [jnp.where(is_f, gc[2 * j], gc[2 * j + 1]) for j in range(nb)]
        beta2 = [jnp.where(is_f, beta_b[2 * j], beta_b[2 * j + 1]) for j in range(nb)]
        decay2 = _each(lambda a, b_: jnp.where(incl2, jnp.exp(jnp.where(incl2, a - b_, 0.0)), 0.0), gc2, gr2)
        lm2 = _each(lambda bt, kq_, dc: jnp.where(strict2, bt * kq_[:c] * dc, 0.0), beta2, kq, decay2)
        yield
        tmat2 = yield from _tri_inverse_pairs(lm2, eye2, pair_mask, level_masks, bd_mask)
        qkm2 = _each(lambda kq_, dc: jnp.where(incl2, kq_[c:] * dc, 0.0), kq, decay2)
        half = lambda x2, d: x2[:, d * c:(d + 1) * c].astype(BF16)
        rhs = _each(lambda j, bt, eg: jnp.concatenate([k[j] * bt * eg, v[j] * bt], axis=1).astype(BF16),
                    pj, beta_b, exp_g)
        wu = _each(lambda j, d, r: _dot(half(tmat2[j], d), r).astype(BF16), pj, pd, rhs)
        yield
        ao = _each(lambda j, d, w_: _dot(half(qkm2[j], d), w_), pj, pd, wu)
        kd = _each(lambda j, gt, g: (k[j] * jnp.exp(gt - g)).astype(BF16), pj, gtot, gc)
        kbm = _each(_dot_tn, kd, wu)
        yield
        for p, (j, d) in enumerate(probs):
            kn_ref, bn_ref, a_ref, o0_ref, gl_ref = outs[d]
            kn_ref[0, pl.ds(2 * r0[j], 2 * c), :] = kbm[p][:, :HEAD_DIM].astype(BF16)
            bn_ref[0, pl.ds(2 * r0[j], 2 * c), :] = kbm[p][:, HEAD_DIM:].astype(BF16)
            a_ref[0, pl.ds(r0[j], c), :] = (q[j] * exp_g[p] - ao[p][:, :HEAD_DIM]).astype(BF16)
            o0_ref[0, pl.ds(r0[j], c), :] = ao[p][:, HEAD_DIM:].astype(BF16)
            gl_ref[0, pl.ds(ci[j] * 8, 8), :] = jnp.broadcast_to(jnp.exp(gtot[p]), (8, HEAD_DIM))

    pending = [chunk_set(list(range(s0, s0 + nb))) for s0 in range(0, PREP_NC, nb)]
    active = []
    while pending or active:
        if pending and (not active or active[-1][1] >= PREP_START_GAP):
            active.append([pending.pop(0), 0])
        for entry in list(active):
            try:
                next(entry[0])
                entry[1] += 1
            except StopIteration:
                active.remove(entry)


def _dn_prep(proj3, ab3, conv_h, a_log, dt_bias):
    b, seq, _ = proj3.shape
    seg = PREP_SEG
    ns = seq // seg
    hb16 = seg // 16
    last16 = seq // 16 - 1
    gate_params = jnp.zeros((8, AB_COLS), F32)
    gate_params = gate_params.at[0, :2 * DN_HEADS].set(a_log.reshape(-1)).at[1, :2 * DN_HEADS].set(dt_bias.reshape(-1))
    tok = lambda bi, h, s: (bi, s, h)
    c0 = COL_DN // DN_HEAD_COLS
    big = jax.ShapeDtypeStruct((b, 2 * seq, DN_WIDTH), BF16)
    med = jax.ShapeDtypeStruct((b, seq, DN_WIDTH), BF16)
    gls = jax.ShapeDtypeStruct((b, seq // DN_CHUNK * 8, DN_WIDTH), F32)
    per_dir_specs = [pl.BlockSpec((1, 2 * seg, HEAD_DIM), tok), pl.BlockSpec((1, 2 * seg, HEAD_DIM), tok),
                     pl.BlockSpec((1, seg, HEAD_DIM), tok), pl.BlockSpec((1, seg, HEAD_DIM), tok),
                     pl.BlockSpec((1, PREP_NC * 8, HEAD_DIM), tok)]
    return pl.pallas_call(
        _dn_prep_kernel,
        grid=(b, DN_HEADS, ns),
        in_specs=[pl.BlockSpec((8, AB_COLS), lambda bi, h, s: (0, 0)),
                  pl.BlockSpec((1, seg, DN_HEAD_COLS), lambda bi, h, s: (bi, s, c0 + h)),
                  pl.BlockSpec((1, 16, DN_HEAD_COLS), lambda bi, h, s: (bi, jnp.maximum(s * hb16 - 1, 0), c0 + h)),
                  pl.BlockSpec((1, 16, DN_HEAD_COLS),
                               lambda bi, h, s: (bi, jnp.minimum((s + 1) * hb16, last16), c0 + h)),
                  pl.BlockSpec((1, seg, AB_COLS), lambda bi, h, s: (bi, s, 0)),
                  pl.BlockSpec((1, 8, DN_HEAD_COLS), lambda bi, h, s: (h, 0, 0))],
        out_specs=per_dir_specs + per_dir_specs,
        out_shape=[big, big, med, med, gls] * 2,
        scratch_shapes=[pltpu.VMEM((seg + 32, DN_HEAD_COLS), BF16), pltpu.VMEM((seg, AB_COLS), F32)],
        compiler_params=_cparams(("parallel", "parallel", "parallel")),
    )(gate_params, proj3, proj3, proj3, ab3, conv_h)


SCAN_HB = 4


def _dn_scan_kernel(knf_ref, bnf_ref, af_ref, o0f_ref, glf_ref, knb_ref, bnb_ref, ab_ref, o0b_ref, glb_ref,
                    of_ref, ob_ref, s_scr, *, nc):
    c = DN_CHUNK

    @pl.when(pl.program_id(2) == 0)
    def _():
        s_scr[...] = jnp.zeros(s_scr.shape, F32)

    def chain(idx, ci, kn_ref, bn_ref, a_ref, o0_ref, gl_ref, o_ref, hh):
        sl = slice(hh * HEAD_DIM, (hh + 1) * HEAD_DIM)
        r2 = pl.multiple_of(ci * 2 * c, 2 * c)
        r1 = pl.multiple_of(ci * c, c)
        lhs = jnp.concatenate([kn_ref[0, pl.ds(r2, 2 * c), sl], a_ref[0, pl.ds(r1, c), sl]], axis=0)
        st = s_scr[idx]
        t1 = _dot(lhs, st.astype(BF16))
        gl = gl_ref[0, pl.ds(pl.multiple_of(ci * 8, 8), 8), sl][0:1, :]
        s_scr[idx] = gl * st - t1[:2 * c] + bn_ref[0, pl.ds(r2, 2 * c), sl].astype(F32)
        o_ref[0, pl.ds(r1, c), sl] = (t1[2 * c:] + o0_ref[0, pl.ds(r1, c), sl].astype(F32)).astype(o_ref.dtype)

    def body(ci, carry):
        cb = nc - 1 - ci
        for hh in range(SCAN_HB):
            chain(hh, ci, knf_ref, bnf_ref, af_ref, o0f_ref, glf_ref, of_ref, hh)
            chain(SCAN_HB + hh, cb, knb_ref, bnb_ref, ab_ref, o0b_ref, glb_ref, ob_ref, hh)
        return carry

    lax.fori_loop(0, nc, body, 0)


def _dn_scan(prep):
    knf, bnf, af, o0f, glf, knb, bnb, ab_, o0b, glb = prep
    b, seq, _ = af.shape
    seg = min(1024, seq)
    nc = seg // DN_CHUNK
    ns = seq // seg
    w = SCAN_HB * HEAD_DIM
    fwd = lambda bi, hb, s: (bi, s, hb)
    bwd = lambda bi, hb, s: (bi, ns - 1 - s, hb)

    def specs(im):
        return [pl.BlockSpec((1, 2 * seg, w), im), pl.BlockSpec((1, 2 * seg, w), im),
                pl.BlockSpec((1, seg, w), im), pl.BlockSpec((1, seg, w), im),
                pl.BlockSpec((1, nc * 8, w), im)]

    out = jax.ShapeDtypeStruct((b, seq, DN_WIDTH), BF16)
    return pl.pallas_call(
        functools.partial(_dn_scan_kernel, nc=nc),
        grid=(b, DN_HEADS // SCAN_HB, ns),
        in_specs=specs(fwd) + specs(bwd),
        out_specs=[pl.BlockSpec((1, seg, w), fwd), pl.BlockSpec((1, seg, w), bwd)],
        out_shape=[out, out],
        scratch_shapes=[pltpu.VMEM((2 * SCAN_HB, HEAD_DIM, HEAD_DIM), F32)],
        compiler_params=_cparams(("parallel", "parallel", "arbitrary")),
    )(knf, bnf, af, o0f, glf, knb, bnb, ab_, o0b, glb)


ATTN_TQ = 512


def _attn_kernel(q_ref, kp_ref, km_ref, kn_ref, vp_ref, vm_ref, vn_ref, o_ref, lse_ref, *, tq):
    i = pl.program_id(2)
    nq = pl.num_programs(2)
    scale = HEAD_DIM ** -0.5
    win = tq + 2 * HALO
    row = lax.broadcasted_iota(jnp.int32, (tq, win), 0)
    colw = lax.broadcasted_iota(jnp.int32, (tq, win), 1)
    mask = jnp.abs(row - (colw - HALO)) <= HALO
    mask = jnp.logical_and(mask, jnp.logical_or(colw >= HALO, i > 0))
    mask = jnp.logical_and(mask, jnp.logical_or(colw < HALO + tq, i < nq - 1))
    lane = lax.broadcasted_iota(jnp.int32, (tq, HEAD_DIM), 1)
    for rb in range(q_ref.shape[1]):
        lse_tile = jnp.zeros((tq, HEAD_DIM), F32)
        for hh in range(HPG):
            sl = slice(hh * HEAD_DIM, (hh + 1) * HEAD_DIM)
            k_all = jnp.concatenate([kp_ref[0, rb, :, sl], km_ref[0, rb, :, sl], kn_ref[0, rb, :, sl]], axis=0)
            v_all = jnp.concatenate([vp_ref[0, rb, :, sl], vm_ref[0, rb, :, sl], vn_ref[0, rb, :, sl]], axis=0)
            s = jnp.where(mask, _dot_nt(q_ref[0, rb, :, sl], k_all) * scale, MASK_VALUE)
            m = jnp.max(s, axis=-1, keepdims=True)
            p = jnp.exp(s - m)
            den = jnp.sum(p, axis=-1, keepdims=True)
            o = _dot(p.astype(BF16), v_all)
            o_ref[0, rb, :, sl] = (o / den).astype(o_ref.dtype)
            lse_tile = jnp.where(lane == hh, m + jnp.log(den), lse_tile)
        lse_ref[0, rb] = lse_tile


def _attention_group(at4, cq, ck, cv):
    b, dil, m, _ = at4.shape
    tq = min(ATTN_TQ, m)
    rbk = min(dil, ATTN_TQ // tq)
    nq = m // tq
    hq = tq // HALO
    lastb = m // HALO - 1

    def main(cb):
        return pl.BlockSpec((1, rbk, tq, ATTN_OUT), lambda bi, r, i: (bi, r, i, cb))

    def prev(cb):
        return pl.BlockSpec((1, rbk, HALO, ATTN_OUT), lambda bi, r, i: (bi, r, jnp.maximum(i * hq - 1, 0), cb))

    def nxt(cb):
        return pl.BlockSpec((1, rbk, HALO, ATTN_OUT), lambda bi, r, i: (bi, r, jnp.minimum((i + 1) * hq, lastb), cb))

    return pl.pallas_call(
        functools.partial(_attn_kernel, tq=tq),
        grid=(b, dil // rbk, nq),
        in_specs=[main(cq), prev(ck), main(ck), nxt(ck), prev(cv), main(cv), nxt(cv)],
        out_specs=[pl.BlockSpec((1, rbk, tq, ATTN_OUT), lambda bi, r, i: (bi, r, i, 0)),
                   pl.BlockSpec((1, rbk, tq, HEAD_DIM), lambda bi, r, i: (bi, r, i, 0))],
        out_shape=[jax.ShapeDtypeStruct((b, dil, m, ATTN_OUT), BF16),
                   jax.ShapeDtypeStruct((b, dil, m, HEAD_DIM), F32)],
        compiler_params=_cparams(("parallel", "parallel", "parallel")),
    )(at4, at4, at4, at4, at4, at4, at4)


MIX_TM = 512
MIX_VMEM_LIMIT = 60 * 1024 * 1024


def _mix_kernel(x_ref, mod_ref, of_ref, ob_ref, z_ref, gdn_ref, gat_ref,
                o1_ref, o2_ref, o3_ref, l1_ref, l2_ref, l3_ref,
                dnw_ref, wdn_ref, wat_ref, wout_ref, npost_ref, out_ref, a_scr, b_scr,
                o2_scr, o3_scr, l2_scr, l3_scr):
    tm = a_scr.shape[0]
    for src, lsrc, dst, ldst, gi in ((o2_ref, l2_ref, o2_scr, l2_scr, 1), (o3_ref, l3_ref, o3_scr, l3_scr, 2)):
        dil = ATTN_GROUPS[gi][1]
        for r in range(dil):
            for hh in range(HPG):
                dst[hh, pl.ds(r, tm // dil, stride=dil), :] = (
                    src[0, r, :, hh * HEAD_DIM:(hh + 1) * HEAD_DIM].astype(F32))
            ldst[pl.ds(r, tm // dil, stride=dil), :] = lsrc[0, r]
    dnw = dnw_ref[...]
    for h in range(DN_HEADS):
        sl = slice(h * HEAD_DIM, (h + 1) * HEAD_DIM)
        o = of_ref[:, sl].astype(F32) + ob_ref[:, sl].astype(F32)
        y = o * lax.rsqrt(jnp.mean(o * o, axis=-1, keepdims=True) + NORM_EPS) * dnw
        a_scr[:, sl] = (y * _silu(z_ref[:, sl].astype(F32))).astype(BF16)
    l1 = l1_ref[...]
    l2 = l2_scr[...]
    l3 = l3_scr[...]
    lm = jnp.maximum(l1, jnp.maximum(l2, l3))
    e1 = jnp.exp(l1 - lm)
    e2 = jnp.exp(l2 - lm)
    e3 = jnp.exp(l3 - lm)
    inv = 1.0 / (e1 + e2 + e3)
    w1 = e1 * inv
    w2 = e2 * inv
    w3 = e3 * inv
    for h in range(HPG):
        sl = slice(h * HEAD_DIM, (h + 1) * HEAD_DIM)
        b_scr[:, sl] = (w1[:, h:h + 1] * o1_ref[:, sl].astype(F32) + w2[:, h:h + 1] * o2_scr[h]
                        + w3[:, h:h + 1] * o3_scr[h]).astype(BF16)
    y_dn = _dot(a_scr[...], wdn_ref[...])
    y_at = _dot(b_scr[...], wat_ref[...])
    mixed_in = (jax.nn.sigmoid(gdn_ref[...].astype(F32)) * y_dn
                + jax.nn.sigmoid(gat_ref[...].astype(F32)) * y_at).astype(BF16)
    mixed = _dot(mixed_in, wout_ref[...])
    nrm = mixed * lax.rsqrt(jnp.mean(mixed * mixed, axis=-1, keepdims=True) + NORM_EPS) * npost_ref[...]
    out_ref[...] = x_ref[...] + mod_ref[0, 2:3, :] * nrm


def _mix(x2, mod3, proj2, o_f, o_b, at_o, at_lse, dn_norm_w, w_dn_out, w_at_out, w_out, norm_post, seq):
    rows = x2.shape[0]
    tm = MIX_TM
    nt = seq // tm
    rowblk = lambda w: pl.BlockSpec((tm, w), lambda i: (i, 0))
    const = lambda shape: pl.BlockSpec(shape, lambda i: (0, 0), pipeline_mode=pl.Buffered(1))
    d1, d2 = ATTN_GROUPS[1][1], ATTN_GROUPS[2][1]
    dilblk = lambda dil, w: pl.BlockSpec((1, dil, tm // dil, w), lambda i: (i // nt, 0, i % nt, 0))
    return pl.pallas_call(
        _mix_kernel,
        grid=(rows // tm,),
        in_specs=[rowblk(D_MODEL),
                  pl.BlockSpec((1, N_MOD, D_MODEL), lambda i: (i // nt, 0, 0)),
                  rowblk(DN_WIDTH), rowblk(DN_WIDTH),
                  pl.BlockSpec((tm, DN_WIDTH), lambda i: (i, COL_Z // DN_WIDTH)),
                  pl.BlockSpec((tm, D_MODEL), lambda i: (i, COL_MERGE // D_MODEL)),
                  pl.BlockSpec((tm, D_MODEL), lambda i: (i, COL_MERGE // D_MODEL + 1)),
                  rowblk(ATTN_OUT), dilblk(d1, ATTN_OUT), dilblk(d2, ATTN_OUT),
                  rowblk(HEAD_DIM), dilblk(d1, HEAD_DIM), dilblk(d2, HEAD_DIM),
                  const((1, HEAD_DIM)), const((DN_WIDTH, D_MODEL)), const((ATTN_OUT, D_MODEL)),
                  const((D_MODEL, D_MODEL)), const((1, D_MODEL))],
        out_specs=rowblk(D_MODEL),
        out_shape=jax.ShapeDtypeStruct((rows, D_MODEL), F32),
        scratch_shapes=[pltpu.VMEM((tm, DN_WIDTH), BF16), pltpu.VMEM((tm, ATTN_OUT), BF16),
                        pltpu.VMEM((HPG, tm, HEAD_DIM), F32), pltpu.VMEM((HPG, tm, HEAD_DIM), F32),
                        pltpu.VMEM((tm, HEAD_DIM), F32), pltpu.VMEM((tm, HEAD_DIM), F32)],
        compiler_params=_cparams(("parallel",), MIX_VMEM_LIMIT),
    )(x2, mod3, o_f, o_b, proj2, proj2, proj2, at_o[0].reshape(rows, ATTN_OUT), at_o[1], at_o[2],
      at_lse[0].reshape(rows, HEAD_DIM), at_lse[1], at_lse[2],
      dn_norm_w.reshape(1, HEAD_DIM), w_dn_out, w_at_out, w_out, norm_post.reshape(1, D_MODEL))


FFN_TM = 1024
FFN_TF = 512


def _ffn_kernel(x_ref, mod_ref, npre_ref, w1_ref, w2_ref, npost_ref, out_ref, h_scr):
    j = pl.program_id(1)

    @pl.when(j == 0)
    def _():
        x = x_ref[...]
        y = x * lax.rsqrt(jnp.mean(x * x, axis=-1, keepdims=True) + NORM_EPS) * npre_ref[...]
        h_scr[...] = (y * (1.0 + mod_ref[0, 4:5, :]) + mod_ref[0, 3:4, :]).astype(BF16)
        out_ref[...] = jnp.zeros(out_ref.shape, F32)

    t = jnp.maximum(_dot(h_scr[...], w1_ref[...]), 0.0)
    out_ref[...] += _dot((t * t).astype(BF16), w2_ref[...])

    @pl.when(j == pl.num_programs(1) - 1)
    def _():
        f = out_ref[...]
        nrm = f * lax.rsqrt(jnp.mean(f * f, axis=-1, keepdims=True) + NORM_EPS) * npost_ref[...]
        out_ref[...] = x_ref[...] + mod_ref[0, 5:6, :] * nrm


def _ffn(x2, mod3, norm_pre, w_ff1, w_ff2, norm_post, seq):
    rows = x2.shape[0]
    tm = min(FFN_TM, seq)
    nt = seq // tm
    return pl.pallas_call(
        _ffn_kernel,
        grid=(rows // tm, D_FF // FFN_TF),
        in_specs=[pl.BlockSpec((tm, D_MODEL), lambda i, j: (i, 0)),
                  pl.BlockSpec((1, N_MOD, D_MODEL), lambda i, j: (i // nt, 0, 0)),
                  pl.BlockSpec((1, D_MODEL), lambda i, j: (0, 0)),
                  pl.BlockSpec((D_MODEL, FFN_TF), lambda i, j: (0, j)),
                  pl.BlockSpec((FFN_TF, D_MODEL), lambda i, j: (j, 0)),
                  pl.BlockSpec((1, D_MODEL), lambda i, j: (0, 0))],
        out_specs=pl.BlockSpec((tm, D_MODEL), lambda i, j: (i, 0)),
        out_shape=jax.ShapeDtypeStruct((rows, D_MODEL), F32),
        scratch_shapes=[pltpu.VMEM((tm, D_MODEL), BF16)],
        compiler_params=_cparams(("parallel", "arbitrary")),
    )(x2, mod3, norm_pre.reshape(1, D_MODEL), w_ff1, w_ff2, norm_post.reshape(1, D_MODEL))


def _prep_in_weights(w_in, conv_w):
    w_in = w_in.astype(BF16)
    o = 0
    dn_qkv = w_in[:, o:o + 3 * DN_WIDTH]; o += 3 * DN_WIDTH
    dn_z = w_in[:, o:o + DN_WIDTH]; o += DN_WIDTH
    dn_ab = w_in[:, o:o + 4 * DN_HEADS]; o += 4 * DN_HEADS
    at_qkv = w_in[:, o:o + 3 * ATTN_WIDTH]; o += 3 * ATTN_WIDTH
    merge = w_in[:, o:o + 2 * D_MODEL]
    dn_ph = dn_qkv.reshape(D_MODEL, 3, DN_HEADS, HEAD_DIM).transpose(0, 2, 1, 3).reshape(D_MODEL, 3 * DN_WIDTH)
    at_pg = (at_qkv.reshape(D_MODEL, 3, N_GROUPS, ATTN_OUT).transpose(0, 2, 1, 3)
             .reshape(D_MODEL, 3 * ATTN_WIDTH))
    at0 = at_pg[:, :AT_GROUP_COLS]
    w_main = jnp.concatenate([at_pg[:, AT_GROUP_COLS:], at0[:, :2 * ATTN_OUT], dn_z, merge, dn_ph,
                              at0[:, 2 * ATTN_OUT:]], axis=1).astype(BF16)
    w_ab = jnp.pad(dn_ab, ((0, 0), (0, AB_COLS - 4 * DN_HEADS))).astype(BF16)
    conv_h = conv_w.reshape(DN_CONV, 3, DN_HEADS, HEAD_DIM).transpose(2, 0, 1, 3).reshape(DN_HEADS, DN_CONV, DN_HEAD_COLS)
    conv_h = jnp.pad(conv_h, ((0, 0), (0, 8 - DN_CONV), (0, 0)))
    return w_main, w_ab, conv_h


def _rope_tables(seq):
    half = HEAD_DIM // 2
    inv_freq = ROPE_THETA ** (-jnp.arange(half, dtype=F32) / half)
    ang = jnp.arange(seq, dtype=F32)[:, None] * inv_freq[None, :]
    cos = jnp.cos(ang)
    sin = jnp.sin(ang)
    return jnp.concatenate([cos, cos], axis=1), jnp.concatenate([-sin, sin], axis=1)


def _group_forward(x, mod, wts):
    b, seq, _ = x.shape
    rows = b * seq
    x2 = x.reshape(rows, D_MODEL)
    mod3 = mod.reshape(b, N_MOD, D_MODEL)
    cos_t, sin_t = _rope_tables(seq)
    proj2, at1, at2, ab2 = _in_proj(x2, mod3, wts["norm_pre_mix"], wts["w_main"], wts["w_ab"], cos_t, sin_t, seq)
    proj3 = proj2.reshape(b, seq, W_PROJ)
    prep = _dn_prep(proj3, ab2.reshape(b, seq, AB_COLS), wts["conv_h"], wts["A_log"], wts["dt_bias"])
    o_f, o_b = _dn_scan(prep)
    at = [_attention_group(proj2.reshape(b, 1, seq, W_PROJ), COL_AT0_QK // ATTN_OUT, COL_AT0_QK // ATTN_OUT + 1,
                           COL_AT0_V // ATTN_OUT),
          _attention_group(at1, 0, 1, 2), _attention_group(at2, 0, 1, 2)]
    x1 = _mix(x2, mod3, proj2, o_f.reshape(rows, DN_WIDTH), o_b.reshape(rows, DN_WIDTH),
              [a[0] for a in at], [a[1] for a in at], wts["dn_norm_w"], wts["w_dn_out"], wts["w_at_out"],
              wts["w_out"], wts["norm_post_mix"], seq)
    y = _ffn(x1, mod3, wts["norm_pre_ffn"], wts["w_ff1"], wts["w_ff2"], wts["norm_post_ffn"], seq)
    return y.reshape(b, seq, D_MODEL)


def kernel(x_prompt, x_sample, c_prompt, c_sample, w_ada, b_ada, norm_pre_mix, norm_post_mix, norm_pre_ffn,
           norm_post_ffn, w_in, conv_w, A_log, dt_bias, dn_norm_w, w_dn_out, w_at_out, w_out, w_ff1, w_ff2):
    xs = (x_prompt, x_sample)
    nbp = c_prompt.shape[0]
    c_all = jnp.concatenate([c_prompt, c_sample], axis=0)
    for l in range(w_ada.shape[0]):
        w_main, w_ab, conv_h = _prep_in_weights(w_in[l], conv_w[l])
        wts = dict(norm_pre_mix=norm_pre_mix[l], norm_post_mix=norm_post_mix[l], norm_pre_ffn=norm_pre_ffn[l],
                   norm_post_ffn=norm_post_ffn[l], w_main=w_main, w_ab=w_ab, conv_h=conv_h, A_log=A_log[l],
                   dt_bias=dt_bias[l], dn_norm_w=dn_norm_w[l], w_dn_out=w_dn_out[l].astype(BF16),
                   w_at_out=w_at_out[l].astype(BF16), w_out=w_out[l].astype(BF16),
                   w_ff1=w_ff1[l].astype(BF16), w_ff2=w_ff2[l].astype(BF16))
        mod_all = _ada_mod(c_all, w_ada[l], b_ada[l])
        xs = (_group_forward(xs[0], mod_all[:nbp], wts), _group_forward(xs[1], mod_all[nbp:], wts))
    return xs
```

```python
import functools

import jax
import jax.numpy as jnp
from jax import lax
from jax.experimental import pallas as pl
from jax.experimental.pallas import tpu as pltpu

F32 = jnp.float32
BF16 = jnp.bfloat16

D_MODEL = 2048
HEAD_DIM = 128
DN_HEADS = 8
DN_WIDTH = DN_HEADS * HEAD_DIM
DN_CONV = 5
DN_CHUNK = 64
ATTN_GROUPS = ((128, 1), (512, 4), (2048, 16))
N_GROUPS = 3
HPG = 4
ATTN_HEADS = N_GROUPS * HPG
ATTN_WIDTH = ATTN_HEADS * HEAD_DIM
ATTN_OUT = HPG * HEAD_DIM
ROPE_THETA = 10000.0
D_FF = 4 * D_MODEL
N_MOD = 6
NORM_EPS = 1e-6
MASK_VALUE = -1e30

DN_HEAD_COLS = 3 * HEAD_DIM
AT_GROUP_COLS = 3 * ATTN_OUT
COL_AT0_QK = 0
COL_Z = COL_AT0_QK + 2 * ATTN_OUT
COL_MERGE = COL_Z + DN_WIDTH
COL_DN = COL_MERGE + 2 * D_MODEL
COL_AT0_V = COL_DN + 3 * DN_WIDTH
W_PROJ = COL_AT0_V + ATTN_OUT
AB_COLS = 128

PROJ_TM = 1024
PROJ_TN = 512
DIL_TILES = 2 * AT_GROUP_COLS // PROJ_TN
MAIN_TILES = W_PROJ // PROJ_TN
ROT_TILES = DIL_TILES + COL_Z // PROJ_TN
W_MAIN = (DIL_TILES + MAIN_TILES) * PROJ_TN
HALO = 64

VMEM_LIMIT = 56 * 1024 * 1024


def _cparams(sem, vmem_limit=VMEM_LIMIT):
    return pltpu.CompilerParams(dimension_semantics=sem, vmem_limit_bytes=vmem_limit)


def _dot(a, b):
    return jnp.dot(a, b, preferred_element_type=F32)


def _dot_nt(a, b):
    return lax.dot_general(a, b, (((1,), (1,)), ((), ())), preferred_element_type=F32)


def _dot_tn(a, b):
    return lax.dot_general(a, b, (((0,), (0,)), ((), ())), preferred_element_type=F32)


def _silu(x):
    return x * jax.nn.sigmoid(x)


def _mod_kernel(c_ref, w_ref, b_ref, o_ref):
    s = _silu(c_ref[...]).astype(BF16)
    o_ref[...] = _dot(s, w_ref[...].astype(BF16)) + b_ref[...]


def _ada_mod(c_all, w_ada, b_ada):
    nb, n = c_all.shape[0], w_ada.shape[1]
    tn = 512
    return pl.pallas_call(
        _mod_kernel,
        grid=(n // tn,),
        in_specs=[pl.BlockSpec((nb, D_MODEL), lambda j: (0, 0)),
                  pl.BlockSpec((D_MODEL, tn), lambda j: (0, j)),
                  pl.BlockSpec((1, tn), lambda j: (0, j))],
        out_specs=pl.BlockSpec((nb, tn), lambda j: (0, j)),
        out_shape=jax.ShapeDtypeStruct((nb, n), F32),
        compiler_params=_cparams(("arbitrary",)),
    )(c_all, w_ada, b_ada.reshape(1, n))


def _inproj_kernel(x_ref, mod_ref, nw_ref, w_ref, wab_ref, cos_ref, sin_ref, o_ref, o1_ref, o2_ref, ab_ref,
                   h_scr, acc_scr):
    j = pl.program_id(1)
    tm = acc_scr.shape[1]
    nslab = PROJ_TN // HEAD_DIM

    @pl.when(j == 0)
    def _():
        x = x_ref[...]
        ms = jnp.mean(x * x, axis=-1, keepdims=True)
        y = x * lax.rsqrt(ms + NORM_EPS) * nw_ref[...]
        h = y * (1.0 + mod_ref[0, 1:2, :]) + mod_ref[0, 0:1, :]
        hb = h.astype(BF16)
        h_scr[...] = hb
        ab_ref[...] = _dot(hb, wab_ref[...])

    acc = _dot(h_scr[...], w_ref[...])
    is_rot = jnp.logical_and(j < ROT_TILES, lax.rem(j, 3) != 2)
    is_dil = j < DIL_TILES

    def rotated(a):
        return a * cos_ref[...] + pltpu.roll(a, HEAD_DIM // 2, axis=1) * sin_ref[...]

    o_ref[...] = acc.astype(BF16)

    @pl.when(jnp.logical_and(jnp.logical_not(is_dil), is_rot))
    def _():
        for hh in range(nslab):
            sl = slice(hh * HEAD_DIM, (hh + 1) * HEAD_DIM)
            o_ref[:, sl] = rotated(acc[:, sl]).astype(BF16)

    @pl.when(jnp.logical_and(is_dil, is_rot))
    def _():
        for hh in range(nslab):
            acc_scr[hh] = rotated(acc[:, hh * HEAD_DIM:(hh + 1) * HEAD_DIM])

    @pl.when(jnp.logical_and(is_dil, jnp.logical_not(is_rot)))
    def _():
        for hh in range(nslab):
            acc_scr[hh] = acc[:, hh * HEAD_DIM:(hh + 1) * HEAD_DIM]

    for o_dil, gi in ((o1_ref, 1), (o2_ref, 2)):
        dil = ATTN_GROUPS[gi][1]
        t0 = 3 * (gi - 1)

        @pl.when(jnp.logical_and(j >= t0, j < t0 + 3))
        def _(o_dil=o_dil, dil=dil):
            for r in range(dil):
                for hh in range(nslab):
                    o_dil[0, r, :, hh * HEAD_DIM:(hh + 1) * HEAD_DIM] = (
                        acc_scr[hh, pl.ds(r, tm // dil, stride=dil), :].astype(BF16))


def _in_proj(x2, mod3, norm_w, w_main, w_ab, cos_t, sin_t, seq):
    rows = x2.shape[0]
    b = rows // seq
    tm = min(PROJ_TM, seq)
    nt = seq // tm
    d1, d2 = ATTN_GROUPS[1][1], ATTN_GROUPS[2][1]

    def dil_spec(dil, t0):
        return pl.BlockSpec((1, dil, tm // dil, PROJ_TN),
                            lambda i, j: (i // nt, 0, i % nt, jnp.clip(j - t0, 0, 2)))

    def rope_block(i, j):
        return (i % nt, 0)

    return pl.pallas_call(
        _inproj_kernel,
        grid=(rows // tm, W_MAIN // PROJ_TN),
        in_specs=[pl.BlockSpec((tm, D_MODEL), lambda i, j: (i, 0)),
                  pl.BlockSpec((1, N_MOD, D_MODEL), lambda i, j: (i // nt, 0, 0)),
                  pl.BlockSpec((1, D_MODEL), lambda i, j: (0, 0)),
                  pl.BlockSpec((D_MODEL, PROJ_TN), lambda i, j: (0, j)),
                  pl.BlockSpec((D_MODEL, AB_COLS), lambda i, j: (0, 0)),
                  pl.BlockSpec((tm, HEAD_DIM), rope_block), pl.BlockSpec((tm, HEAD_DIM), rope_block)],
        out_specs=[pl.BlockSpec((tm, PROJ_TN), lambda i, j: (i, jnp.maximum(j - DIL_TILES, 0))),
                   dil_spec(d1, 0), dil_spec(d2, 3),
                   pl.BlockSpec((tm, AB_COLS), lambda i, j: (i, 0))],
        out_shape=[jax.ShapeDtypeStruct((rows, W_PROJ), BF16),
                   jax.ShapeDtypeStruct((b, d1, seq // d1, AT_GROUP_COLS), BF16),
                   jax.ShapeDtypeStruct((b, d2, seq // d2, AT_GROUP_COLS), BF16),
                   jax.ShapeDtypeStruct((rows, AB_COLS), F32)],
        scratch_shapes=[pltpu.VMEM((tm, D_MODEL), BF16), pltpu.VMEM((PROJ_TN // HEAD_DIM, tm, HEAD_DIM), F32)],
        compiler_params=_cparams(("parallel", "arbitrary")),
    )(x2, mod3, norm_w.reshape(1, D_MODEL), w_main, w_ab, cos_t, sin_t)


PREP_SEG = 2048
PREP_NC = PREP_SEG // DN_CHUNK
PREP_SET = 4
PREP_START_GAP = 2


def _each(f, *lists):
    return [f(*args) for args in zip(*lists)]


def _tri_inverse_pairs(lms, eye2, pair_mask, level_masks, bd_mask):
    l_bd = _each(lambda lm: jnp.concatenate([lm, lm], axis=0), lms)
    xs = _each(lambda lm: eye2 - jnp.where(pair_mask, lm, 0.0), lms)
    for mask in level_masks:
        n_bd = _each(lambda lb: jnp.where(mask, lb, 0.0).astype(BF16), l_bd)
        ys = _each(lambda x, n: _dot(x.astype(BF16), n), xs, n_bd)
        yield
        x_bd = _each(lambda x: jnp.where(bd_mask, jnp.concatenate([x, x], axis=0), 0.0).astype(BF16), xs)
        xs = _each(lambda x, y, xb: x - _dot(y.astype(BF16), xb), xs, ys, x_bd)
        yield
    return xs


def _dn_prep_kernel(gp_ref, main_ref, prev_ref, next_ref, ab_ref, cw_ref,
                    knf_ref, bnf_ref, af_ref, o0f_ref, glf_ref,
                    knb_ref, bnb_ref, ab_out_ref, o0b_ref, glb_ref, xe_scr, gb_scr):
    h = pl.program_id(1)
    s = pl.program_id(2)
    ns = pl.num_programs(2)
    c = DN_CHUNK
    seg = PREP_SEG

    xe_scr[0:16, :] = prev_ref[0] * (s > 0).astype(BF16)
    xe_scr[16:seg + 16, :] = main_ref[0]
    xe_scr[seg + 16:seg + 32, :] = next_ref[0] * (s < ns - 1).astype(BF16)
    srow = lax.broadcasted_iota(jnp.int32, (4 * c, c + 32), 0)
    scol = lax.broadcasted_iota(jnp.int32, (4 * c, c + 32), 1)
    stap = jnp.right_shift(srow, 6)
    stap = stap + (stap >= 2).astype(jnp.int32)
    shift_sel = (scol == jnp.bitwise_and(srow, c - 1) + 14 + stap).astype(BF16)

    row = lax.broadcasted_iota(jnp.int32, (c, 2 * c), 0)
    lane = lax.broadcasted_iota(jnp.int32, (c, 2 * c), 1)
    col = jnp.bitwise_and(lane, c - 1)
    is_f = lane < c
    is_b = jnp.logical_not(is_f)
    incl2 = jnp.logical_or(jnp.logical_and(is_f, row >= col), jnp.logical_and(is_b, row <= col))
    strict2 = jnp.logical_or(jnp.logical_and(is_f, row > col), jnp.logical_and(is_b, row < col))
    eye2 = (row == col).astype(F32)
    tri_cat = ((col <= row).astype(BF16), (col >= row).astype(BF16))
    gr_mask = (jnp.logical_and(is_f, row <= col), jnp.logical_and(is_b, row >= col))
    ones4 = jnp.ones((c, 4 * c), BF16)
    row128 = lax.broadcasted_iota(jnp.int32, (2 * c, 2 * c), 0)
    lane128 = lax.broadcasted_iota(jnp.int32, (2 * c, 2 * c), 1)
    bd_mask = jnp.bitwise_and(row128, c) == jnp.bitwise_and(lane128, c)
    r64 = jnp.bitwise_and(row128, c - 1)
    c64 = jnp.bitwise_and(lane128, c - 1)

    def joins(ri, ci_, lg):
        return jnp.logical_and(jnp.right_shift(ri, lg) != jnp.right_shift(ci_, lg),
                               jnp.right_shift(ri, lg + 1) == jnp.right_shift(ci_, lg + 1))

    pair_mask = joins(row, col, 0)
    level_masks = [jnp.logical_and(bd_mask, joins(r64, c64, lg)) for lg in range(1, 6)]
    ab_all = ab_ref[0]
    z = ab_all + gp_ref[1:2, :]
    softplus = jnp.maximum(z, 0.0) + jnp.log(1.0 + jnp.exp(-jnp.abs(z)))
    lane_seg = lax.broadcasted_iota(jnp.int32, (seg, AB_COLS), 1)
    gb_scr[...] = jnp.where(lane_seg < 2 * DN_HEADS, -jnp.exp(gp_ref[0:1, :]) * softplus, jax.nn.sigmoid(ab_all))
    cw = cw_ref[0]
    outs = ((knf_ref, bnf_ref, af_ref, o0f_ref, glf_ref), (knb_ref, bnb_ref, ab_out_ref, o0b_ref, glb_ref))

    nb = PREP_SET
    probs = [(j, d) for j in range(nb) for d in range(2)]
    pj = [j for j, _ in probs]
    pd = [d for _, d in probs]

    def chunk_set(ci):
        r0 = [cj * c for cj in ci]
        wins = [xe_scr[pl.ds(r0[j], c + 32), :] for j in range(nb)]
        shifted = _each(lambda w_: _dot(shift_sel, w_), wins)

        def conv_silu(w_, sh):
            acc = cw[2:3, :] * w_[16:16 + c, :].astype(F32)
            for ti, t in enumerate((0, 1, 3, 4)):
                acc = acc + cw[t:t + 1, :] * sh[ti * c:(ti + 1) * c, :]
            return _silu(acc)

        xs = _each(conv_silu, wins, shifted)
        yield
        q = [x[:, :HEAD_DIM] for x in xs]
        k = [x[:, HEAD_DIM:2 * HEAD_DIM] for x in xs]
        v = [x[:, 2 * HEAD_DIM:] for x in xs]
        q = _each(lambda t: t * lax.rsqrt(jnp.sum(t * t, axis=-1, keepdims=True) + NORM_EPS) * (HEAD_DIM ** -0.5), q)
        k = _each(lambda t: t * lax.rsqrt(jnp.sum(t * t, axis=-1, keepdims=True) + NORM_EPS), k)
        qb = _each(lambda t: t.astype(BF16), q)
        kbf = _each(lambda t: t.astype(BF16), k)
        kq = _each(lambda kb_, qb_: _dot_nt(jnp.concatenate([kb_, qb_], axis=0), jnp.concatenate([kb_, kb_], axis=0)),
                   kbf, qb)
        yield
        gbc = [gb_scr[pl.ds(r0[j], c), :] for j in range(nb)]

        def gate_cols(j, d):
            g_col = jnp.sum(jnp.where(lane == d * DN_HEADS + h, gbc[j], 0.0), axis=-1, keepdims=True)
            b_col = jnp.sum(jnp.where(lane == 2 * DN_HEADS + d * DN_HEADS + h, gbc[j], 0.0), axis=-1, keepdims=True)
            return jnp.broadcast_to(g_col, (c, HEAD_DIM)), jnp.broadcast_to(b_col, (c, HEAD_DIM))

        gates = _each(gate_cols, pj, pd)
        g_b = [g for g, _ in gates]
        beta_b = [bt for _, bt in gates]
        g_hi = _each(lambda g: g.astype(BF16).astype(F32), g_b)
        g_lo = _each(lambda g, gh: g - gh, g_b, g_hi)
        gc = _each(lambda d, gh, gl_: _dot(tri_cat[d], jnp.concatenate([gh, gl_], axis=0).astype(BF16)),
                   pd, g_hi, g_lo)
        gtot = _each(lambda g: jnp.sum(g, axis=0, keepdims=True), g_b)
        exp_g = _each(jnp.exp, gc)
        yield

        def row_form(j):
            pf, pb = 2 * j, 2 * j + 1
            parts = [jnp.where(gr_mask[0], g_hi[pf], 0.0), jnp.where(gr_mask[0], g_lo[pf], 0.0),
                     jnp.where(gr_mask[1], g_hi[pb], 0.0), jnp.where(gr_mask[1], g_lo[pb], 0.0)]
            return _dot(ones4, jnp.concatenate(parts, axis=0).astype(BF16))

        gr2 = [row_form(j) for j in range(nb)]
        gc2 = [jnp.where(is_f, gc[2 * j], gc[2 * j + 1]) for j in range(nb)]
        beta2 = [jnp.where(is_f, beta_b[2 * j], beta_b[2 * j + 1]) for j in range(nb)]
        decay2 = _each(lambda a, b_: jnp.where(incl2, jnp.exp(jnp.where(incl2, a - b_, 0.0)), 0.0), gc2, gr2)
        lm2 = _each(lambda bt, kq_, dc: jnp.where(strict2, bt * kq_[:c] * dc, 0.0), beta2, kq, decay2)
        yield
        tmat2 = yield from _tri_inverse_pairs(lm2, eye2, pair_mask, level_masks, bd_mask)
        qkm2 = _each(lambda kq_, dc: jnp.where(incl2, kq_[c:] * dc, 0.0), kq, decay2)
        half = lambda x2, d: x2[:, d * c:(d + 1) * c].astype(BF16)
        rhs = _each(lambda j, bt, eg: jnp.concatenate([k[j] * bt * eg, v[j] * bt], axis=1).astype(BF16),
                    pj, beta_b, exp_g)
        wu = _each(lambda j, d, r: _dot(half(tmat2[j], d), r).astype(BF16), pj, pd, rhs)
        yield
        ao = _each(lambda j, d, w_: _dot(half(qkm2[j], d), w_), pj, pd, wu)
        kd = _each(lambda j, gt, g: (k[j] * jnp.exp(gt - g)).astype(BF16), pj, gtot, gc)
        kbm = _each(_dot_tn, kd, wu)
        yield
        for p, (j, d) in enumerate(probs):
            kn_ref, bn_ref, a_ref, o0_ref, gl_ref = outs[d]
            kn_ref[0, pl.ds(2 * r0[j], 2 * c), :] = kbm[p][:, :HEAD_DIM].astype(BF16)
            bn_ref[0, pl.ds(2 * r0[j], 2 * c), :] = kbm[p][:, HEAD_DIM:].astype(BF16)
            a_ref[0, pl.ds(r0[j], c), :] = (q[j] * exp_g[p] - ao[p][:, :HEAD_DIM]).astype(BF16)
            o0_ref[0, pl.ds(r0[j], c), :] = ao[p][:, HEAD_DIM:].astype(BF16)
            gl_ref[0, pl.ds(ci[j] * 8, 8), :] = jnp.broadcast_to(jnp.exp(gtot[p]), (8, HEAD_DIM))

    pending = [chunk_set(list(range(s0, s0 + nb))) for s0 in range(0, PREP_NC, nb)]
    active = []
    while pending or active:
        if pending and (not active or active[-1][1] >= PREP_START_GAP):
            active.append([pending.pop(0), 0])
        for entry in list(active):
            try:
                next(entry[0])
                entry[1] += 1
            except StopIteration:
                active.remove(entry)


def _dn_prep(proj3, ab3, conv_h, a_log, dt_bias):
    b, seq, _ = proj3.shape
    seg = PREP_SEG
    ns = seq // seg
    hb16 = seg // 16
    last16 = seq // 16 - 1
    gate_params = jnp.zeros((8, AB_COLS), F32)
    gate_params = gate_params.at[0, :2 * DN_HEADS].set(a_log.reshape(-1)).at[1, :2 * DN_HEADS].set(dt_bias.reshape(-1))
    tok = lambda bi, h, s: (bi, s, h)
    c0 = COL_DN // DN_HEAD_COLS
    big = jax.ShapeDtypeStruct((b, 2 * seq, DN_WIDTH), BF16)
    med = jax.ShapeDtypeStruct((b, seq, DN_WIDTH), BF16)
    gls = jax.ShapeDtypeStruct((b, seq // DN_CHUNK * 8, DN_WIDTH), F32)
    per_dir_specs = [pl.BlockSpec((1, 2 * seg, HEAD_DIM), tok), pl.BlockSpec((1, 2 * seg, HEAD_DIM), tok),
                     pl.BlockSpec((1, seg, HEAD_DIM), tok), pl.BlockSpec((1, seg, HEAD_DIM), tok),
                     pl.BlockSpec((1, PREP_NC * 8, HEAD_DIM), tok)]
    return pl.pallas_call(
        _dn_prep_kernel,
        grid=(b, DN_HEADS, ns),
        in_specs=[pl.BlockSpec((8, AB_COLS), lambda bi, h, s: (0, 0)),
                  pl.BlockSpec((1, seg, DN_HEAD_COLS), lambda bi, h, s: (bi, s, c0 + h)),
                  pl.BlockSpec((1, 16, DN_HEAD_COLS), lambda bi, h, s: (bi, jnp.maximum(s * hb16 - 1, 0), c0 + h)),
                  pl.BlockSpec((1, 16, DN_HEAD_COLS),
                               lambda bi, h, s: (bi, jnp.minimum((s + 1) * hb16, last16), c0 + h)),
                  pl.BlockSpec((1, seg, AB_COLS), lambda bi, h, s: (bi, s, 0)),
                  pl.BlockSpec((1, 8, DN_HEAD_COLS), lambda bi, h, s: (h, 0, 0))],
        out_specs=per_dir_specs + per_dir_specs,
        out_shape=[big, big, med, med, gls] * 2,
        scratch_shapes=[pltpu.VMEM((seg + 32, DN_HEAD_COLS), BF16), pltpu.VMEM((seg, AB_COLS), F32)],
        compiler_params=_cparams(("parallel", "parallel", "parallel")),
    )(gate_params, proj3, proj3, proj3, ab3, conv_h)


SCAN_HB = 8
SCAN_SEG = 512


def _dn_scan_kernel(knf_ref, bnf_ref, af_ref, o0f_ref, glf_ref, knb_ref, bnb_ref, ab_ref, o0b_ref, glb_ref,
                    of_ref, ob_ref, s_scr, *, nc):
    c = DN_CHUNK

    @pl.when(pl.program_id(2) == 0)
    def _():
        s_scr[...] = jnp.zeros(s_scr.shape, F32)

    def chain(idx, ci, kn_ref, bn_ref, a_ref, o0_ref, gl_ref, o_ref, hh):
        sl = slice(hh * HEAD_DIM, (hh + 1) * HEAD_DIM)
        r2 = pl.multiple_of(ci * 2 * c, 2 * c)
        r1 = pl.multiple_of(ci * c, c)
        lhs = jnp.concatenate([kn_ref[0, pl.ds(r2, 2 * c), sl], a_ref[0, pl.ds(r1, c), sl]], axis=0)
        st = s_scr[idx]
        t1 = _dot(lhs, st.astype(BF16))
        gl = gl_ref[0, pl.ds(pl.multiple_of(ci * 8, 8), 8), sl][0:1, :]
        s_scr[idx] = gl * st - t1[:2 * c] + bn_ref[0, pl.ds(r2, 2 * c), sl].astype(F32)
        o_ref[0, pl.ds(r1, c), sl] = (t1[2 * c:] + o0_ref[0, pl.ds(r1, c), sl].astype(F32)).astype(o_ref.dtype)

    def body(ci, carry):
        cb = nc - 1 - ci
        for hh in range(SCAN_HB):
            chain(hh, ci, knf_ref, bnf_ref, af_ref, o0f_ref, glf_ref, of_ref, hh)
            chain(SCAN_HB + hh, cb, knb_ref, bnb_ref, ab_ref, o0b_ref, glb_ref, ob_ref, hh)
        return carry

    lax.fori_loop(0, nc, body, 0)


def _dn_scan(prep):
    knf, bnf, af, o0f, glf, knb, bnb, ab_, o0b, glb = prep
    b, seq, _ = af.shape
    seg = min(SCAN_SEG, seq)
    nc = seg // DN_CHUNK
    ns = seq // seg
    w = SCAN_HB * HEAD_DIM
    fwd = lambda bi, hb, s: (bi, s, hb)
    bwd = lambda bi, hb, s: (bi, ns - 1 - s, hb)

    def specs(im):
        return [pl.BlockSpec((1, 2 * seg, w), im), pl.BlockSpec((1, 2 * seg, w), im),
                pl.BlockSpec((1, seg, w), im), pl.BlockSpec((1, seg, w), im),
                pl.BlockSpec((1, nc * 8, w), im)]

    out = jax.ShapeDtypeStruct((b, seq, DN_WIDTH), BF16)
    return pl.pallas_call(
        functools.partial(_dn_scan_kernel, nc=nc),
        grid=(b, DN_HEADS // SCAN_HB, ns),
        in_specs=specs(fwd) + specs(bwd),
        out_specs=[pl.BlockSpec((1, seg, w), fwd), pl.BlockSpec((1, seg, w), bwd)],
        out_shape=[out, out],
        scratch_shapes=[pltpu.VMEM((2 * SCAN_HB, HEAD_DIM, HEAD_DIM), F32)],
        compiler_params=_cparams(("parallel", "parallel", "arbitrary")),
    )(knf, bnf, af, o0f, glf, knb, bnb, ab_, o0b, glb)


ATTN_TQ = 512


def _attn_kernel(q_ref, kp_ref, km_ref, kn_ref, vp_ref, vm_ref, vn_ref, o_ref, lse_ref, *, tq):
    i = pl.program_id(2)
    nq = pl.num_programs(2)
    scale = HEAD_DIM ** -0.5
    win = tq + 2 * HALO
    row = lax.broadcasted_iota(jnp.int32, (tq, win), 0)
    colw = lax.broadcasted_iota(jnp.int32, (tq, win), 1)
    mask = jnp.abs(row - (colw - HALO)) <= HALO
    mask = jnp.logical_and(mask, jnp.logical_or(colw >= HALO, i > 0))
    mask = jnp.logical_and(mask, jnp.logical_or(colw < HALO + tq, i < nq - 1))
    lane = lax.broadcasted_iota(jnp.int32, (tq, HEAD_DIM), 1)
    for rb in range(q_ref.shape[1]):
        lse_tile = jnp.zeros((tq, HEAD_DIM), F32)
        for hh in range(HPG):
            sl = slice(hh * HEAD_DIM, (hh + 1) * HEAD_DIM)
            k_all = jnp.concatenate([kp_ref[0, rb, :, sl], km_ref[0, rb, :, sl], kn_ref[0, rb, :, sl]], axis=0)
            v_all = jnp.concatenate([vp_ref[0, rb, :, sl], vm_ref[0, rb, :, sl], vn_ref[0, rb, :, sl]], axis=0)
            s = jnp.where(mask, _dot_nt(q_ref[0, rb, :, sl], k_all) * scale, MASK_VALUE)
            m = jnp.max(s, axis=-1, keepdims=True)
            p = jnp.exp(s - m)
            den = jnp.sum(p, axis=-1, keepdims=True)
            o = _dot(p.astype(BF16), v_all)
            o_ref[0, rb, :, sl] = (o / den).astype(o_ref.dtype)
            lse_tile = jnp.where(lane == hh, m + jnp.log(den), lse_tile)
        lse_ref[0, rb] = lse_tile


def _attention_group(at4, cq, ck, cv):
    b, dil, m, _ = at4.shape
    tq = min(ATTN_TQ, m)
    rbk = min(dil, ATTN_TQ // tq)
    nq = m // tq
    hq = tq // HALO
    lastb = m // HALO - 1

    def main(cb):
        return pl.BlockSpec((1, rbk, tq, ATTN_OUT), lambda bi, r, i: (bi, r, i, cb))

    def prev(cb):
        return pl.BlockSpec((1, rbk, HALO, ATTN_OUT), lambda bi, r, i: (bi, r, jnp.maximum(i * hq - 1, 0), cb))

    def nxt(cb):
        return pl.BlockSpec((1, rbk, HALO, ATTN_OUT), lambda bi, r, i: (bi, r, jnp.minimum((i + 1) * hq, lastb), cb))

    return pl.pallas_call(
        functools.partial(_attn_kernel, tq=tq),
        grid=(b, dil // rbk, nq),
        in_specs=[main(cq), prev(ck), main(ck), nxt(ck), prev(cv), main(cv), nxt(cv)],
        out_specs=[pl.BlockSpec((1, rbk, tq, ATTN_OUT), lambda bi, r, i: (bi, r, i, 0)),
                   pl.BlockSpec((1, rbk, tq, HEAD_DIM), lambda bi, r, i: (bi, r, i, 0))],
        out_shape=[jax.ShapeDtypeStruct((b, dil, m, ATTN_OUT), BF16),
                   jax.ShapeDtypeStruct((b, dil, m, HEAD_DIM), F32)],
        compiler_params=_cparams(("parallel", "parallel", "parallel")),
    )(at4, at4, at4, at4, at4, at4, at4)


MIX_TM = 512
MIX_VMEM_LIMIT = 60 * 1024 * 1024


def _mix_kernel(x_ref, mod_ref, of_ref, ob_ref, z_ref, gdn_ref, gat_ref,
                o1_ref, o2_ref, o3_ref, l1_ref, l2_ref, l3_ref,
                dnw_ref, wdn_ref, wat_ref, wout_ref, npost_ref, out_ref, a_scr, b_scr,
                o2_scr, o3_scr, l2_scr, l3_scr):
    tm = a_scr.shape[0]
    for src, lsrc, dst, ldst, gi in ((o2_ref, l2_ref, o2_scr, l2_scr, 1), (o3_ref, l3_ref, o3_scr, l3_scr, 2)):
        dil = ATTN_GROUPS[gi][1]
        for r in range(dil):
            for hh in range(HPG):
                dst[hh, pl.ds(r, tm // dil, stride=dil), :] = (
                    src[0, r, :, hh * HEAD_DIM:(hh + 1) * HEAD_DIM].astype(F32))
            ldst[pl.ds(r, tm // dil, stride=dil), :] = lsrc[0, r]
    dnw = dnw_ref[...]
    for h in range(DN_HEADS):
        sl = slice(h * HEAD_DIM, (h + 1) * HEAD_DIM)
        o = of_ref[:, sl].astype(F32) + ob_ref[:, sl].astype(F32)
        y = o * lax.rsqrt(jnp.mean(o * o, axis=-1, keepdims=True) + NORM_EPS) * dnw
        a_scr[:, sl] = (y * _silu(z_ref[:, sl].astype(F32))).astype(BF16)
    l1 = l1_ref[...]
    l2 = l2_scr[...]
    l3 = l3_scr[...]
    lm = jnp.maximum(l1, jnp.maximum(l2, l3))
    e1 = jnp.exp(l1 - lm)
    e2 = jnp.exp(l2 - lm)
    e3 = jnp.exp(l3 - lm)
    inv = 1.0 / (e1 + e2 + e3)
    w1 = e1 * inv
    w2 = e2 * inv
    w3 = e3 * inv
    for h in range(HPG):
        sl = slice(h * HEAD_DIM, (h + 1) * HEAD_DIM)
        b_scr[:, sl] = (w1[:, h:h + 1] * o1_ref[:, sl].astype(F32) + w2[:, h:h + 1] * o2_scr[h]
                        + w3[:, h:h + 1] * o3_scr[h]).astype(BF16)
    y_dn = _dot(a_scr[...], wdn_ref[...])
    y_at = _dot(b_scr[...], wat_ref[...])
    mixed_in = (jax.nn.sigmoid(gdn_ref[...].astype(F32)) * y_dn
                + jax.nn.sigmoid(gat_ref[...].astype(F32)) * y_at).astype(BF16)
    mixed = _dot(mixed_in, wout_ref[...])
    nrm = mixed * lax.rsqrt(jnp.mean(mixed * mixed, axis=-1, keepdims=True) + NORM_EPS) * npost_ref[...]
    out_ref[...] = x_ref[...] + mod_ref[0, 2:3, :] * nrm


def _mix(x2, mod3, proj2, o_f, o_b, at_o, at_lse, dn_norm_w, w_dn_out, w_at_out, w_out, norm_post, seq):
    rows = x2.shape[0]
    tm = MIX_TM
    nt = seq // tm
    rowblk = lambda w: pl.BlockSpec((tm, w), lambda i: (i, 0))
    const = lambda shape: pl.BlockSpec(shape, lambda i: (0, 0), pipeline_mode=pl.Buffered(1))
    d1, d2 = ATTN_GROUPS[1][1], ATTN_GROUPS[2][1]
    dilblk = lambda dil, w: pl.BlockSpec((1, dil, tm // dil, w), lambda i: (i // nt, 0, i % nt, 0))
    return pl.pallas_call(
        _mix_kernel,
        grid=(rows // tm,),
        in_specs=[rowblk(D_MODEL),
                  pl.BlockSpec((1, N_MOD, D_MODEL), lambda i: (i // nt, 0, 0)),
                  rowblk(DN_WIDTH), rowblk(DN_WIDTH),
                  pl.BlockSpec((tm, DN_WIDTH), lambda i: (i, COL_Z // DN_WIDTH)),
                  pl.BlockSpec((tm, D_MODEL), lambda i: (i, COL_MERGE // D_MODEL)),
                  pl.BlockSpec((tm, D_MODEL), lambda i: (i, COL_MERGE // D_MODEL + 1)),
                  rowblk(ATTN_OUT), dilblk(d1, ATTN_OUT), dilblk(d2, ATTN_OUT),
                  rowblk(HEAD_DIM), dilblk(d1, HEAD_DIM), dilblk(d2, HEAD_DIM),
                  const((1, HEAD_DIM)), const((DN_WIDTH, D_MODEL)), const((ATTN_OUT, D_MODEL)),
                  const((D_MODEL, D_MODEL)), const((1, D_MODEL))],
        out_specs=rowblk(D_MODEL),
        out_shape=jax.ShapeDtypeStruct((rows, D_MODEL), F32),
        scratch_shapes=[pltpu.VMEM((tm, DN_WIDTH), BF16), pltpu.VMEM((tm, ATTN_OUT), BF16),
                        pltpu.VMEM((HPG, tm, HEAD_DIM), F32), pltpu.VMEM((HPG, tm, HEAD_DIM), F32),
                        pltpu.VMEM((tm, HEAD_DIM), F32), pltpu.VMEM((tm, HEAD_DIM), F32)],
        compiler_params=_cparams(("parallel",), MIX_VMEM_LIMIT),
    )(x2, mod3, o_f, o_b, proj2, proj2, proj2, at_o[0].reshape(rows, ATTN_OUT), at_o[1], at_o[2],
      at_lse[0].reshape(rows, HEAD_DIM), at_lse[1], at_lse[2],
      dn_norm_w.reshape(1, HEAD_DIM), w_dn_out, w_at_out, w_out, norm_post.reshape(1, D_MODEL))


FFN_TM = 1024
FFN_TF = 512


def _ffn_kernel(x_ref, mod_ref, npre_ref, w1_ref, w2_ref, npost_ref, out_ref, h_scr):
    j = pl.program_id(1)

    @pl.when(j == 0)
    def _():
        x = x_ref[...]
        y = x * lax.rsqrt(jnp.mean(x * x, axis=-1, keepdims=True) + NORM_EPS) * npre_ref[...]
        h_scr[...] = (y * (1.0 + mod_ref[0, 4:5, :]) + mod_ref[0, 3:4, :]).astype(BF16)
        out_ref[...] = jnp.zeros(out_ref.shape, F32)

    t = jnp.maximum(_dot(h_scr[...], w1_ref[...]), 0.0)
    out_ref[...] += _dot((t * t).astype(BF16), w2_ref[...])

    @pl.when(j == pl.num_programs(1) - 1)
    def _():
        f = out_ref[...]
        nrm = f * lax.rsqrt(jnp.mean(f * f, axis=-1, keepdims=True) + NORM_EPS) * npost_ref[...]
        out_ref[...] = x_ref[...] + mod_ref[0, 5:6, :] * nrm


def _ffn(x2, mod3, norm_pre, w_ff1, w_ff2, norm_post, seq):
    rows = x2.shape[0]
    tm = min(FFN_TM, seq)
    nt = seq // tm
    return pl.pallas_call(
        _ffn_kernel,
        grid=(rows // tm, D_FF // FFN_TF),
        in_specs=[pl.BlockSpec((tm, D_MODEL), lambda i, j: (i, 0)),
                  pl.BlockSpec((1, N_MOD, D_MODEL), lambda i, j: (i // nt, 0, 0)),
                  pl.BlockSpec((1, D_MODEL), lambda i, j: (0, 0)),
                  pl.BlockSpec((D_MODEL, FFN_TF), lambda i, j: (0, j)),
                  pl.BlockSpec((FFN_TF, D_MODEL), lambda i, j: (j, 0)),
                  pl.BlockSpec((1, D_MODEL), lambda i, j: (0, 0))],
        out_specs=pl.BlockSpec((tm, D_MODEL), lambda i, j: (i, 0)),
        out_shape=jax.ShapeDtypeStruct((rows, D_MODEL), F32),
        scratch_shapes=[pltpu.VMEM((tm, D_MODEL), BF16)],
        compiler_params=_cparams(("parallel", "arbitrary")),
    )(x2, mod3, norm_pre.reshape(1, D_MODEL), w_ff1, w_ff2, norm_post.reshape(1, D_MODEL))


def _prep_in_weights(w_in, conv_w):
    w_in = w_in.astype(BF16)
    o = 0
    dn_qkv = w_in[:, o:o + 3 * DN_WIDTH]; o += 3 * DN_WIDTH
    dn_z = w_in[:, o:o + DN_WIDTH]; o += DN_WIDTH
    dn_ab = w_in[:, o:o + 4 * DN_HEADS]; o += 4 * DN_HEADS
    at_qkv = w_in[:, o:o + 3 * ATTN_WIDTH]; o += 3 * ATTN_WIDTH
    merge = w_in[:, o:o + 2 * D_MODEL]
    dn_ph = dn_qkv.reshape(D_MODEL, 3, DN_HEADS, HEAD_DIM).transpose(0, 2, 1, 3).reshape(D_MODEL, 3 * DN_WIDTH)
    at_pg = (at_qkv.reshape(D_MODEL, 3, N_GROUPS, ATTN_OUT).transpose(0, 2, 1, 3)
             .reshape(D_MODEL, 3 * ATTN_WIDTH))
    at0 = at_pg[:, :AT_GROUP_COLS]
    w_main = jnp.concatenate([at_pg[:, AT_GROUP_COLS:], at0[:, :2 * ATTN_OUT], dn_z, merge, dn_ph,
                              at0[:, 2 * ATTN_OUT:]], axis=1).astype(BF16)
    w_ab = jnp.pad(dn_ab, ((0, 0), (0, AB_COLS - 4 * DN_HEADS))).astype(BF16)
    conv_h = conv_w.reshape(DN_CONV, 3, DN_HEADS, HEAD_DIM).transpose(2, 0, 1, 3).reshape(DN_HEADS, DN_CONV, DN_HEAD_COLS)
    conv_h = jnp.pad(conv_h, ((0, 0), (0, 8 - DN_CONV), (0, 0)))
    return w_main, w_ab, conv_h


def _rope_tables(seq):
    half = HEAD_DIM // 2
    inv_freq = ROPE_THETA ** (-jnp.arange(half, dtype=F32) / half)
    ang = jnp.arange(seq, dtype=F32)[:, None] * inv_freq[None, :]
    cos = jnp.cos(ang)
    sin = jnp.sin(ang)
    return jnp.concatenate([cos, cos], axis=1), jnp.concatenate([-sin, sin], axis=1)


def _group_forward(x, mod, wts):
    b, seq, _ = x.shape
    rows = b * seq
    x2 = x.reshape(rows, D_MODEL)
    mod3 = mod.reshape(b, N_MOD, D_MODEL)
    cos_t, sin_t = _rope_tables(seq)
    proj2, at1, at2, ab2 = _in_proj(x2, mod3, wts["norm_pre_mix"], wts["w_main"], wts["w_ab"], cos_t, sin_t, seq)
    proj3 = proj2.reshape(b, seq, W_PROJ)
    prep = _dn_prep(proj3, ab2.reshape(b, seq, AB_COLS), wts["conv_h"], wts["A_log"], wts["dt_bias"])
    o_f, o_b = _dn_scan(prep)
    at = [_attention_group(proj2.reshape(b, 1, seq, W_PROJ), COL_AT0_QK // ATTN_OUT, COL_AT0_QK // ATTN_OUT + 1,
                           COL_AT0_V // ATTN_OUT),
          _attention_group(at1, 0, 1, 2), _attention_group(at2, 0, 1, 2)]
    x1 = _mix(x2, mod3, proj2, o_f.reshape(rows, DN_WIDTH), o_b.reshape(rows, DN_WIDTH),
              [a[0] for a in at], [a[1] for a in at], wts["dn_norm_w"], wts["w_dn_out"], wts["w_at_out"],
              wts["w_out"], wts["norm_post_mix"], seq)
    y = _ffn(x1, mod3, wts["norm_pre_ffn"], wts["w_ff1"], wts["w_ff2"], wts["norm_post_ffn"], seq)
    return y.reshape(b, seq, D_MODEL)


def kernel(x_prompt, x_sample, c_prompt, c_sample, w_ada, b_ada, norm_pre_mix, norm_post_mix, norm_pre_ffn,
           norm_post_ffn, w_in, conv_w, A_log, dt_bias, dn_norm_w, w_dn_out, w_at_out, w_out, w_ff1, w_ff2):
    xs = (x_prompt, x_sample)
    nbp = c_prompt.shape[0]
    c_all = jnp.concatenate([c_prompt, c_sample], axis=0)
    for l in range(w_ada.shape[0]):
        w_main, w_ab, conv_h = _prep_in_weights(w_in[l], conv_w[l])
        wts = dict(norm_pre_mix=norm_pre_mix[l], norm_post_mix=norm_post_mix[l], norm_pre_ffn=norm_pre_ffn[l],
                   norm_post_ffn=norm_post_ffn[l], w_main=w_main, w_ab=w_ab, conv_h=conv_h, A_log=A_log[l],
                   dt_bias=dt_bias[l], dn_norm_w=dn_norm_w[l], w_dn_out=w_dn_out[l].astype(BF16),
                   w_at_out=w_at_out[l].astype(BF16), w_out=w_out[l].astype(BF16),
                   w_ff1=w_ff1[l].astype(BF16), w_ff2=w_ff2[l].astype(BF16))
        mod_all = _ada_mod(c_all, w_ada[l], b_ada[l])
        xs = (_group_forward(xs[0], mod_all[:nbp], wts), _group_forward(xs[1], mod_all[nbp:], wts))
    return xs
```
